```python
import jax, jax.numpy as jnp
from jax import lax
import numpy as np

D_MODEL = 1024
BATCH = 32
SEQ = 2048
DEPTH = 2

GRID_W = 64
CTX_LEN = 256
Q_BLOCK = 128
ROPE_BASE = 10000.0
EPS = 1e-6
NEG_INF = -1e30

MLA_HEADS = 8
MLA_Q_RANK = 256
MLA_KV_RANK = 128
MLA_NOPE = 64
MLA_ROPE = 32
MLA_V = 64
SWA_HEADS = 8
SWA_KV_HEADS = 2
SWA_DIM = 64
SWA_WINDOW = 128
NA_HEADS = 8
NA_DIM = 64
NA_KH = 8
NA_KW = 16
GQA_HEADS = 4
GQA_KV_HEADS = 2
GQA_DIM = 128
D_FF = 2816
CONV_W = 3

EVEN_MLA_IN = MLA_Q_RANK + MLA_KV_RANK + MLA_ROPE
EVEN_SWA_IN = (SWA_HEADS + 2 * SWA_KV_HEADS) * SWA_DIM
EVEN_IN = EVEN_MLA_IN + EVEN_SWA_IN
EVEN_MIX = MLA_HEADS * MLA_V + SWA_HEADS * SWA_DIM
ODD_NA_IN = 3 * NA_HEADS * NA_DIM
ODD_GQA_IN = (GQA_HEADS + 2 * GQA_KV_HEADS) * GQA_DIM
ODD_IN = ODD_NA_IN + ODD_GQA_IN
ODD_MIX = NA_HEADS * NA_DIM + GQA_HEADS * GQA_DIM

kernel_name = 'hybrid_mla_swa_natten_gqa_convffn_prefix'


def rms_norm(x, g):
    xf = x.astype(jnp.float32)
    y = xf * lax.rsqrt(jnp.mean(xf * xf, axis=-1, keepdims=True) + EPS)
    return (y * g.astype(jnp.float32)).astype(x.dtype)


def modulate(x, g, shift, scale):
    return rms_norm(x, g) * (1 + scale) + shift


def rope_tables(n, d):
    t = jnp.arange(n)
    rows = (t // GRID_W).astype(jnp.float32)
    cols = (t % GRID_W).astype(jnp.float32)
    freqs = ROPE_BASE ** (-jnp.arange(d // 4, dtype=jnp.float32) * 4.0 / d)
    ang = jnp.concatenate([rows[:, None] * freqs, cols[:, None] * freqs], axis=-1)
    return jnp.cos(ang), jnp.sin(ang)


def apply_rope(x, cos, sin):
    h = x.shape[-1] // 2
    x1, x2 = x[..., :h], x[..., h:]
    cos = cos.astype(x.dtype)
    sin = sin.astype(x.dtype)
    return jnp.concatenate([x1 * cos - x2 * sin, x2 * cos + x1 * sin], axis=-1)


def heads(p, n_heads):
    b, n, _ = p.shape
    return p.reshape(b, n, n_heads, -1).transpose(0, 2, 1, 3)


def merge_heads(o):
    b, h, n, d = o.shape
    return o.transpose(0, 2, 1, 3).reshape(b, n, h * d)


def group_heads(q, n_kv):
    b, h, n, d = q.shape
    return q.reshape(b, n_kv, h // n_kv, n, d)


def split_qkv(p, nq_heads, nkv_heads, d):
    nq, nk = nq_heads * d, nkv_heads * d
    return (heads(p[..., :nq], nq_heads), heads(p[..., nq:nq + nk], nkv_heads),
            heads(p[..., nq + nk:], nkv_heads))


def softmax_with_sink(s, sink):
    sk = jnp.broadcast_to(sink.astype(jnp.float32)[None, :, :, None, None], s.shape[:-1] + (1,))
    return jax.nn.softmax(jnp.concatenate([s, sk], axis=-1), axis=-1)[..., :-1]


def dense_ctx_attention(q, kc, vc, scale, sink=None):
    s = jnp.einsum('bgrqd,bgkd->bgrqk', q, kc).astype(jnp.float32) * scale
    p = jax.nn.softmax(s, axis=-1) if sink is None else softmax_with_sink(s, sink)
    o = jnp.einsum('bgrqk,bgkd->bgrqd', p.astype(vc.dtype), vc)
    b, g, r, n, dv = o.shape
    return o.reshape(b, g * r, n, dv)


def blocked_global_attention(q, k, v, kc, vc, scale):
    b, g, r, s, d = q.shape
    nb = s // Q_BLOCK
    k_all = jnp.concatenate([kc, k], axis=2)
    v_all = jnp.concatenate([vc, v], axis=2)
    qb = q.reshape(b, g, r, nb, Q_BLOCK, d).transpose(3, 0, 1, 2, 4, 5)

    def block(qi):
        sc = jnp.einsum('bgrqd,bgkd->bgrqk', qi, k_all).astype(jnp.float32) * scale
        p = jax.nn.softmax(sc, axis=-1).astype(v_all.dtype)
        return jnp.einsum('bgrqk,bgkd->bgrqd', p, v_all)

    o = lax.map(block, qb)
    return o.transpose(1, 2, 3, 0, 4, 5).reshape(b, g * r, s, -1)


def banded_window_attention(q, k, v, kc, vc, sink, scale):
    b, g, r, s, d = q.shape
    nb = s // Q_BLOCK
    band = Q_BLOCK + 2 * SWA_WINDOW
    n_ctx = kc.shape[2]
    kp = jnp.pad(k, ((0, 0), (0, 0), (SWA_WINDOW, SWA_WINDOW), (0, 0)))
    vp = jnp.pad(v, ((0, 0), (0, 0), (SWA_WINDOW, SWA_WINDOW), (0, 0)))
    rel = jnp.arange(band)[None, :] - SWA_WINDOW - jnp.arange(Q_BLOCK)[:, None]
    in_window = jnp.abs(rel) <= SWA_WINDOW
    qb = q.reshape(b, g, r, nb, Q_BLOCK, d).transpose(3, 0, 1, 2, 4, 5)

    def block(args):
        qi, n = args
        start = n * Q_BLOCK
        kb = lax.dynamic_slice_in_dim(kp, start, band, axis=2)
        vb = lax.dynamic_slice_in_dim(vp, start, band, axis=2)
        kpos = start - SWA_WINDOW + jnp.arange(band)
        mask = in_window & ((kpos >= 0) & (kpos < s))[None, :]
        s_loc = jnp.einsum('bgrqd,bgkd->bgrqk', qi, kb).astype(jnp.float32) * scale
        s_loc = jnp.where(mask, s_loc, NEG_INF)
        s_ctx = jnp.einsum('bgrqd,bgkd->bgrqk', qi, kc).astype(jnp.float32) * scale
        p = softmax_with_sink(jnp.concatenate([s_ctx, s_loc], axis=-1), sink).astype(vb.dtype)
        return (jnp.einsum('bgrqk,bgkd->bgrqd', p[..., :n_ctx], vc)
                + jnp.einsum('bgrqk,bgkd->bgrqd', p[..., n_ctx:], vb))

    o = lax.map(block, (qb, jnp.arange(nb)))
    return o.transpose(1, 2, 3, 0, 4, 5).reshape(b, g * r, s, -1)


def mla_mixer(p_lat, p_ctx, q_norm_g, w_uq, kv_norm_g, w_ukv, need_ctx):
    def project(p):
        cq = p[..., :MLA_Q_RANK]
        ckv = p[..., MLA_Q_RANK:MLA_Q_RANK + MLA_KV_RANK]
        k_pe = p[..., MLA_Q_RANK + MLA_KV_RANK:]
        q = heads(rms_norm(cq, q_norm_g) @ w_uq, MLA_HEADS)
        kv = heads(rms_norm(ckv, kv_norm_g) @ w_ukv, MLA_HEADS)
        return q[..., :MLA_NOPE], q[..., MLA_NOPE:], kv[..., :MLA_NOPE], kv[..., MLA_NOPE:], k_pe[:, None]

    def assemble(q_nope, q_pe, k_nope, k_pe):
        q = jnp.concatenate([q_nope, q_pe], axis=-1)
        k = jnp.concatenate([k_nope, jnp.broadcast_to(k_pe, k_nope.shape[:-1] + (MLA_ROPE,))], axis=-1)
        return q[:, :, None], k

    q_nope, q_pe, k_nope, v, k_pe = project(p_lat)
    cos, sin = rope_tables(p_lat.shape[1], MLA_ROPE)
    q, k = assemble(q_nope, apply_rope(q_pe, cos, sin), k_nope, apply_rope(k_pe, cos, sin))
    qc_nope, qc_pe, kc_nope, vc, kc_pe = project(p_ctx)
    qc, kc = assemble(qc_nope, qc_pe, kc_nope, kc_pe)
    scale = (MLA_NOPE + MLA_ROPE) ** -0.5
    o = blocked_global_attention(q, k, v, kc, vc, scale)
    o_ctx = merge_heads(dense_ctx_attention(qc, kc, vc, scale)) if need_ctx else None
    return merge_heads(o), o_ctx


def swa_mixer(p_lat, p_ctx, sink, need_ctx):
    q, k, v = split_qkv(p_lat, SWA_HEADS, SWA_KV_HEADS, SWA_DIM)
    qc, kc, vc = split_qkv(p_ctx, SWA_HEADS, SWA_KV_HEADS, SWA_DIM)
    cos, sin = rope_tables(q.shape[2], SWA_DIM)
    q, k = apply_rope(q, cos, sin), apply_rope(k, cos, sin)
    sink_g = sink.reshape(SWA_KV_HEADS, SWA_HEADS // SWA_KV_HEADS)
    scale = SWA_DIM ** -0.5
    o = banded_window_attention(group_heads(q, SWA_KV_HEADS), k, v, kc, vc, sink_g, scale)
    o_ctx = (merge_heads(dense_ctx_attention(group_heads(qc, SWA_KV_HEADS), kc, vc, scale, sink_g))
             if need_ctx else None)
    return merge_heads(o), o_ctx


def na_mixer(p_lat, p_ctx, rpb, need_ctx):
    q, k, v = split_qkv(p_lat, NA_HEADS, NA_HEADS, NA_DIM)
    qc, kc, vc = split_qkv(p_ctx, NA_HEADS, NA_HEADS, NA_DIM)
    b, h, s, d = q.shape
    rows = s // GRID_W
    kh = min(NA_KH, rows)
    scale = d ** -0.5
    n_ctx = kc.shape[2]
    qg = q.reshape(b, h, rows, GRID_W, d)
    kg = k.reshape(b, h, rows, GRID_W, d)
    vg = v.reshape(b, h, rows, GRID_W, d)
    cols = jnp.arange(GRID_W)
    col_start = jnp.clip(cols - NA_KW // 2, 0, GRID_W - NA_KW)
    col_mask = (cols[None, :] >= col_start[:, None]) & (cols[None, :] < col_start[:, None] + NA_KW)
    mask = jnp.broadcast_to(col_mask[:, None, :], (GRID_W, kh, GRID_W)).reshape(GRID_W, kh * GRID_W)
    dc_idx = jnp.clip(cols[None, :] - cols[:, None], -(NA_KW - 1), NA_KW - 1) + NA_KW - 1

    def row_block(args):
        qi, r = args
        rs = jnp.clip(r - kh // 2, 0, rows - kh)
        kr = lax.dynamic_slice_in_dim(kg, rs, kh, axis=2).reshape(b, h, kh * GRID_W, d)
        vr = lax.dynamic_slice_in_dim(vg, rs, kh, axis=2).reshape(b, h, kh * GRID_W, d)
        dr_idx = rs + jnp.arange(kh) - r + NA_KH - 1
        bias = rpb[:, dr_idx[None, :, None], dc_idx[:, None, :]].reshape(h, GRID_W, kh * GRID_W)
        s_loc = jnp.einsum('bhqd,bhkd->bhqk', qi, kr).astype(jnp.float32) * scale + bias.astype(jnp.float32)
        s_loc = jnp.where(mask, s_loc, NEG_INF)
        s_ctx = jnp.einsum('bhqd,bhkd->bhqk', qi, kc).astype(jnp.float32) * scale
        p = jax.nn.softmax(jnp.concatenate([s_ctx, s_loc], axis=-1), axis=-1).astype(vr.dtype)
        return (jnp.einsum('bhqk,bhkd->bhqd', p[..., :n_ctx], vc)
                + jnp.einsum('bhqk,bhkd->bhqd', p[..., n_ctx:], vr))

    o = lax.map(row_block, (qg.transpose(2, 0, 1, 3, 4), jnp.arange(rows)))
    o = o.transpose(1, 2, 0, 3, 4).reshape(b, h, s, d)
    o_ctx = merge_heads(dense_ctx_attention(qc[:, :, None], kc, vc, scale)) if need_ctx else None
    return merge_heads(o), o_ctx


def gqa_mixer(p_lat, p_ctx, q_g, k_g, need_ctx):
    q, k, v = split_qkv(p_lat, GQA_HEADS, GQA_KV_HEADS, GQA_DIM)
    qc, kc, vc = split_qkv(p_ctx, GQA_HEADS, GQA_KV_HEADS, GQA_DIM)
    q, k = rms_norm(q, q_g), rms_norm(k, k_g)
    qc, kc = rms_norm(qc, q_g), rms_norm(kc, k_g)
    cos, sin = rope_tables(q.shape[2], GQA_DIM)
    q, k = apply_rope(q, cos, sin), apply_rope(k, cos, sin)
    scale = GQA_DIM ** -0.5
    o = blocked_global_attention(group_heads(q, GQA_KV_HEADS), k, v, kc, vc, scale)
    o_ctx = (merge_heads(dense_ctx_attention(group_heads(qc, GQA_KV_HEADS), kc, vc, scale))
             if need_ctx else None)
    return merge_heads(o), o_ctx


def conv_ffn(h, w_up, conv_w, conv_b, w_down):
    u = h @ w_up
    gate, val = u[..., :D_FF], u[..., D_FF:]
    n = h.shape[1]
    half = CONV_W // 2
    gp = jnp.pad(gate, ((0, 0), (half, half), (0, 0)))
    acc = conv_b
    for i in range(CONV_W):
        acc = acc + gp[:, i:i + n] * conv_w[i]
    return (jax.nn.gelu(acc) * val) @ w_down


def setup_inputs(seed: int = 0) -> dict:
    key = jax.random.key(seed)
    ks = jax.random.split(key, 32)
    f32 = jnp.float32
    n_even = (DEPTH + 1) // 2
    n_odd = DEPTH // 2

    def w(k, shape, fan_in):
        return jax.random.normal(k, shape, f32) * (fan_in ** -0.5)

    def gain(k, shape):
        return 1.0 + 0.05 * jax.random.normal(k, shape, f32)

    def small(k, shape, s):
        return s * jax.random.normal(k, shape, f32)

    return {
        'x': jax.random.normal(ks[0], (BATCH, SEQ, D_MODEL), f32),
        'c': jax.random.normal(ks[1], (BATCH, D_MODEL), f32),
        'ctx': jax.random.normal(ks[2], (BATCH, CTX_LEN, D_MODEL), f32),
        'c_ctx': jax.random.normal(ks[3], (D_MODEL,), f32),
        'ada_w': w(ks[4], (DEPTH, D_MODEL, 6 * D_MODEL), D_MODEL),
        'ada_b': small(ks[5], (DEPTH, 6 * D_MODEL), 0.02),
        'pre_mix_g': gain(ks[6], (DEPTH, D_MODEL)),
        'post_mix_g': gain(ks[7], (DEPTH, D_MODEL)),
        'pre_ffn_g': gain(ks[8], (DEPTH, D_MODEL)),
        'post_ffn_g': gain(ks[9], (DEPTH, D_MODEL)),
        'ffn_up': w(ks[10], (DEPTH, D_MODEL, 2 * D_FF), D_MODEL),
        'ffn_conv_w': w(ks[11], (DEPTH, CONV_W, D_FF), CONV_W),
        'ffn_conv_b': small(ks[12], (DEPTH, D_FF), 0.02),
        'ffn_down': w(ks[13], (DEPTH, D_FF, D_MODEL), D_FF),
        'even_w_in': w(ks[14], (n_even, D_MODEL, EVEN_IN), D_MODEL),
        'even_w_out': w(ks[15], (n_even, EVEN_MIX, D_MODEL), EVEN_MIX),
        'mla_q_norm_g': gain(ks[16], (n_even, MLA_Q_RANK)),
        'mla_w_uq': w(ks[17], (n_even, MLA_Q_RANK, MLA_HEADS * (MLA_NOPE + MLA_ROPE)), MLA_Q_RANK),
        'mla_kv_norm_g': gain(ks[18], (n_even, MLA_KV_RANK)),
        'mla_w_ukv': w(ks[19], (n_even, MLA_KV_RANK, MLA_HEADS * (MLA_NOPE + MLA_V)), MLA_KV_RANK),
        'swa_sink': jax.random.normal(ks[20], (n_even, SWA_HEADS), f32),
        'odd_w_in': w(ks[21], (n_odd, D_MODEL, ODD_IN), D_MODEL),
        'odd_w_out': w(ks[22], (n_odd, ODD_MIX, D_MODEL), ODD_MIX),
        'na_rpb': small(ks[23], (n_odd, NA_HEADS, 2 * NA_KH - 1, 2 * NA_KW - 1), 0.1),
        'gqa_q_g': gain(ks[24], (n_odd, GQA_DIM)),
        'gqa_k_g': gain(ks[25], (n_odd, GQA_DIM)),
    }


def reference(x, c, ctx, c_ctx, ada_w, ada_b, pre_mix_g, post_mix_g, pre_ffn_g, post_ffn_g,
              ffn_up, ffn_conv_w, ffn_conv_b, ffn_down, even_w_in, even_w_out, mla_q_norm_g,
              mla_w_uq, mla_kv_norm_g, mla_w_ukv, swa_sink, odd_w_in, odd_w_out, na_rpb,
              gqa_q_g, gqa_k_g):
    sc = jax.nn.silu(c)
    scc = jax.nn.silu(c_ctx)
    for layer in range(DEPTH):
        need_ctx = layer < DEPTH - 1
        mod_lat = [m[:, None, :] for m in jnp.split(sc @ ada_w[layer] + ada_b[layer], 6, axis=-1)]
        mod_ctx = jnp.split(scc @ ada_w[layer] + ada_b[layer], 6, axis=-1)
        h_lat = modulate(x, pre_mix_g[layer], mod_lat[0], mod_lat[1])
        h_ctx = modulate(ctx, pre_mix_g[layer], mod_ctx[0], mod_ctx[1])
        if layer % 2 == 0:
            i = layer // 2
            p_lat = h_lat @ even_w_in[i]
            p_ctx = h_ctx @ even_w_in[i]
            a_lat, a_ctx = mla_mixer(p_lat[..., :EVEN_MLA_IN], p_ctx[..., :EVEN_MLA_IN], mla_q_norm_g[i],
                                     mla_w_uq[i], mla_kv_norm_g[i], mla_w_ukv[i], need_ctx)
            b_lat, b_ctx = swa_mixer(p_lat[..., EVEN_MLA_IN:], p_ctx[..., EVEN_MLA_IN:], swa_sink[i], need_ctx)
            w_out = even_w_out[i]
        else:
            i = layer // 2
            p_lat = h_lat @ odd_w_in[i]
            p_ctx = h_ctx @ odd_w_in[i]
            a_lat, a_ctx = na_mixer(p_lat[..., :ODD_NA_IN], p_ctx[..., :ODD_NA_IN], na_rpb[i], need_ctx)
            b_lat, b_ctx = gqa_mixer(p_lat[..., ODD_NA_IN:], p_ctx[..., ODD_NA_IN:], gqa_q_g[i], gqa_k_g[i],
                                     need_ctx)
            w_out = odd_w_out[i]
        y_lat = jnp.concatenate([a_lat, b_lat], axis=-1) @ w_out
        x = x + mod_lat[2] * rms_norm(y_lat, post_mix_g[layer])
        h = modulate(x, pre_ffn_g[layer], mod_lat[3], mod_lat[4])
        f = conv_ffn(h, ffn_up[layer], ffn_conv_w[layer], ffn_conv_b[layer], ffn_down[layer])
        x = x + mod_lat[5] * rms_norm(f, post_ffn_g[layer])
        if need_ctx:
            y_ctx = jnp.concatenate([a_ctx, b_ctx], axis=-1) @ w_out
            ctx = ctx + mod_ctx[2] * rms_norm(y_ctx, post_mix_g[layer])
            hc = modulate(ctx, pre_ffn_g[layer], mod_ctx[3], mod_ctx[4])
            fc = conv_ffn(hc, ffn_up[layer], ffn_conv_w[layer], ffn_conv_b[layer], ffn_down[layer])
            ctx = ctx + mod_ctx[5] * rms_norm(fc, post_ffn_g[layer])
    return x
```

```python
import functools
import math

import jax
import jax.numpy as jnp
import numpy as np
from jax import lax
from jax.experimental import pallas as pl
from jax.experimental.pallas import tpu as pltpu

F32 = jnp.float32
BF16 = jnp.bfloat16

D_MODEL = 1024
GRID_W = 64
ROPE_BASE = 10000.0
EPS = 1e-6
NEG_INF = -1e30
LANES = 128

MLA_HEADS, MLA_Q_RANK, MLA_KV_RANK, MLA_NOPE, MLA_ROPE, MLA_V = 8, 256, 128, 64, 32, 64
SWA_HEADS, SWA_KV_HEADS, SWA_DIM, SWA_WINDOW = 8, 2, 64, 128
NA_HEADS, NA_DIM, NA_KH, NA_KW = 8, 64, 8, 16
GQA_HEADS, GQA_KV_HEADS, GQA_DIM = 4, 2, 128
D_FF = 2816
FF_CHUNK = 256
N_FF_CHUNKS = D_FF // FF_CHUNK
HALO = 16

NA_TILE_ROWS = 4
NA_WIN_ROWS = NA_TILE_ROWS + NA_KH - 1

VMEM_LIMIT = 56 * 1024 * 1024


def _params():
    return pltpu.CompilerParams(vmem_limit_bytes=VMEM_LIMIT)


def _dot(a, b):
    return jnp.dot(a, b, preferred_element_type=F32)


def _dot_nt(a, b):
    return lax.dot_general(a, b, (((1,), (1,)), ((), ())), preferred_element_type=F32)


def _rms(x, g):
    return x * lax.rsqrt(jnp.mean(x * x, axis=-1, keepdims=True) + EPS) * g


def _rope(x, c, sa, sb, sha, shb):
    return x * c + pltpu.roll(x, sha, axis=1) * sa + pltpu.roll(x, shb, axis=1) * sb


def _lane_lo(shape):
    return lax.broadcasted_iota(jnp.int32, shape, len(shape) - 1) < (LANES // 2)


def _const_spec(shape):
    nd = len(shape)
    return pl.BlockSpec(shape, lambda *_: (0,) * nd)


def _ada_kernel(cc_ref, w_ref, b_ref, o_ref):
    cc = cc_ref[...]
    s = cc * (1.0 / (1.0 + jnp.exp(-cc)))
    o_ref[0] = _dot(s.astype(BF16), w_ref[0].astype(BF16)) + b_ref[0]


def _ada(cc, ada_w, ada_b):
    depth = ada_w.shape[0]
    rows = cc.shape[0]
    return pl.pallas_call(
        _ada_kernel,
        grid=(depth, 6),
        in_specs=[
            pl.BlockSpec((rows, D_MODEL), lambda l, j: (0, 0)),
            pl.BlockSpec((1, D_MODEL, D_MODEL), lambda l, j: (l, 0, j)),
            pl.BlockSpec((1, 1, D_MODEL), lambda l, j: (l, 0, j)),
        ],
        out_specs=pl.BlockSpec((1, rows, D_MODEL), lambda l, j: (l, 0, j)),
        out_shape=jax.ShapeDtypeStruct((depth, rows, 6 * D_MODEL), F32),
        compiler_params=_params(),
        name="ada",
    )(cc, ada_w, ada_b.reshape(depth, 1, 6 * D_MODEL))


class _Rows:
    def __init__(self, n, seq, tm, per_batch_mod):
        assert seq % tm == 0 and n % seq == 0
        self.n, self.seq, self.tm = n, seq, tm
        self.tps = seq // tm
        self.grid = (n // tm,)
        self.per_batch_mod = per_batch_mod

    def row_spec(self, width):
        return pl.BlockSpec((self.tm, width), lambda i: (i, 0))

    def mod_spec(self):
        if self.per_batch_mod:
            tps = self.tps
            return pl.BlockSpec((1, 8, D_MODEL), lambda i: (i // tps, 0, 0))
        return pl.BlockSpec((1, 8, D_MODEL), lambda i: (0, 0, 0))

    def table_spec(self, width):
        tps = self.tps
        return pl.BlockSpec((self.tm, width), lambda i: (i % tps, 0))


def _modulated(x_ref, mod_ref, g_ref, shift_row, scale_row):
    x = x_ref[...]
    shift = mod_ref[0, shift_row:shift_row + 1, :]
    scale = mod_ref[0, scale_row:scale_row + 1, :]
    return _rms(x, g_ref[...]) * (1.0 + scale) + shift


EVEN_COLS = 1280


def _pre_even_kernel(x_ref, mod_ref, g_ref, win_ref, qg_ref, wuq_ref, kvg_ref, wuk_ref, wuv_ref,
                     tab_ref, qm_ref, km_ref, vm_ref, qs_ref, ks_ref, vs_ref):
    h = _modulated(x_ref, mod_ref, g_ref, 0, 1).astype(BF16)
    p = _dot(h, win_ref[...])
    mc, ma, mb = tab_ref[:, 0:128], tab_ref[:, 128:256], tab_ref[:, 256:384]
    sc, sa, sb = tab_ref[:, 384:512], tab_ref[:, 512:640], tab_ref[:, 640:768]

    cq = _rms(p[:, 0:256], qg_ref[...]).astype(BF16)
    q = _dot(cq, wuq_ref[...])
    mla_scale = (MLA_NOPE + MLA_ROPE) ** -0.5
    ckv = _rms(p[:, 256:384], kvg_ref[...]).astype(BF16)
    kn = _dot(ckv, wuk_ref[...])
    kpe = _rope(p[:, 384:512], mc, ma, mb, 112, 16)
    for j in range(MLA_HEADS):
        sl = slice(j * LANES, (j + 1) * LANES)
        qm_ref[:, sl] = (_rope(q[:, sl], mc, ma, mb, 112, 16) * mla_scale).astype(BF16)
        km_ref[:, sl] = (kn[:, sl] + kpe).astype(BF16)
    vm_ref[...] = _dot(ckv, wuv_ref[...]).astype(BF16)

    swa_scale = SWA_DIM ** -0.5
    for j in range(4):
        sl = slice(512 + j * LANES, 512 + (j + 1) * LANES)
        qs_ref[:, j * LANES:(j + 1) * LANES] = (_rope(p[:, sl], sc, sa, sb, 96, 32) * swa_scale).astype(BF16)
    ks_ref[...] = _rope(p[:, 1024:1152], sc, sa, sb, 96, 32).astype(BF16)
    vs_ref[...] = p[:, 1152:1280].astype(BF16)


def _pre_even(rows, x, mod, g, win, qg, wuq, kvg, wuk, wuv, tab):
    n = rows.n
    widths = (1024, 1024, 512, 512, 128, 128)
    return pl.pallas_call(
        _pre_even_kernel,
        grid=rows.grid,
        in_specs=[
            rows.row_spec(D_MODEL), rows.mod_spec(), _const_spec((1, D_MODEL)),
            _const_spec(win.shape), _const_spec(qg.shape), _const_spec(wuq.shape),
            _const_spec(kvg.shape), _const_spec(wuk.shape), _const_spec(wuv.shape),
            rows.table_spec(768),
        ],
        out_specs=[rows.row_spec(w) for w in widths],
        out_shape=[jax.ShapeDtypeStruct((n, w), BF16) for w in widths],
        compiler_params=_params(),
        name="pre_even",
    )(x, mod, g, win, qg, wuq, kvg, wuk, wuv, tab)


def _pre_odd_kernel(x_ref, mod_ref, g_ref, win_ref, qg_ref, kg_ref, tab_ref,
                    qn_ref, kn_ref, vn_ref, qq_ref, kq_ref, vq_ref):
    h = _modulated(x_ref, mod_ref, g_ref, 0, 1).astype(BF16)
    p = _dot(h, win_ref[...])
    gc, gs = tab_ref[:, 0:128], tab_ref[:, 128:256]
    na_scale = NA_DIM ** -0.5
    qn_ref[...] = (p[:, 0:512] * na_scale).astype(BF16)
    kn_ref[...] = p[:, 512:1024].astype(BF16)
    vn_ref[...] = p[:, 1024:1536].astype(BF16)
    gqa_scale = GQA_DIM ** -0.5

    def normed(blk, gain):
        y = _rms(blk, gain)
        return y * gc + pltpu.roll(y, 64, axis=1) * gs

    for j in range(GQA_HEADS):
        sl = slice(1536 + j * LANES, 1536 + (j + 1) * LANES)
        qq_ref[:, j * LANES:(j + 1) * LANES] = (normed(p[:, sl], qg_ref[...]) * gqa_scale).astype(BF16)
    for j in range(GQA_KV_HEADS):
        sl = slice(2048 + j * LANES, 2048 + (j + 1) * LANES)
        kq_ref[:, j * LANES:(j + 1) * LANES] = normed(p[:, sl], kg_ref[...]).astype(BF16)
    vq_ref[...] = p[:, 2304:2560].astype(BF16)


def _pre_odd(rows, x, mod, g, win, qg, kg, tab):
    n = rows.n
    widths = (512, 512, 512, 512, 256, 256)
    return pl.pallas_call(
        _pre_odd_kernel,
        grid=rows.grid,
        in_specs=[
            rows.row_spec(D_MODEL), rows.mod_spec(), _const_spec((1, D_MODEL)),
            _const_spec(win.shape), _const_spec(qg.shape), _const_spec(kg.shape),
            rows.table_spec(256),
        ],
        out_specs=[rows.row_spec(w) for w in widths],
        out_shape=[jax.ShapeDtypeStruct((n, w), BF16) for w in widths],
        compiler_params=_params(),
        name="pre_odd",
    )(x, mod, g, win, qg, kg, tab)


def _attend(q, keys, values, biases=None, sink=None):
    scores = [_dot_nt(q, k) for k in keys]
    if biases is not None:
        scores = [s if b is None else s + b for s, b in zip(scores, biases)]
    m = functools.reduce(jnp.maximum, [jnp.max(s, axis=-1, keepdims=True) for s in scores])
    if sink is not None:
        m = jnp.maximum(m, sink)
    probs = [jnp.exp(s - m) for s in scores]
    l = functools.reduce(jnp.add, [jnp.sum(p, axis=-1, keepdims=True) for p in probs])
    if sink is not None:
        l = l + jnp.exp(sink - m)
    o = functools.reduce(jnp.add, [_dot(p.astype(BF16), v) for p, v in zip(probs, values)])
    return o / l


def _mla_kernel(has_lat, q_ref, kc_ref, vc_ref, *rest):
    if has_lat:
        kl_ref, vl_ref, o_ref = rest
    else:
        (o_ref,) = rest
    outs = []
    for e in range(2):
        sl = slice(e * LANES, (e + 1) * LANES)
        keys, vals = [kc_ref[:, sl]], [vc_ref[...]]
        if has_lat:
            keys.append(kl_ref[:, sl])
            vals.append(vl_ref[...])
        outs.append(_attend(q_ref[:, sl], keys, vals))
    o_ref[...] = jnp.where(_lane_lo(outs[0].shape), outs[0], outs[1]).astype(BF16)


def _mla_attn(q, kc, vc, kl, vl, batch, q_seq, ctx_len, lat_seq, tq):
    has_lat = kl is not None
    nq = q_seq // tq
    in_specs = [
        pl.BlockSpec((tq, 256), lambda b, j, t: (b * nq + t, j)),
        pl.BlockSpec((ctx_len, 256), lambda b, j, t: (b, j)),
        pl.BlockSpec((ctx_len, 128), lambda b, j, t: (b, j)),
    ]
    args = [q, kc, vc]
    if has_lat:
        in_specs += [
            pl.BlockSpec((lat_seq, 256), lambda b, j, t: (b, j)),
            pl.BlockSpec((lat_seq, 128), lambda b, j, t: (b, j)),
        ]
        args += [kl, vl]
    return pl.pallas_call(
        functools.partial(_mla_kernel, has_lat),
        grid=(batch, MLA_HEADS // 2, nq),
        in_specs=in_specs,
        out_specs=pl.BlockSpec((tq, 128), lambda b, j, t: (b * nq + t, j)),
        out_shape=jax.ShapeDtypeStruct((batch * q_seq, MLA_HEADS * MLA_V), BF16),
        compiler_params=_params(),
        name="mla_attn_lat" if has_lat else "mla_attn_ctx",
    )(*args)


def _gqa_kernel(q_ref, kc_ref, vc_ref, kl_ref, vl_ref, o_ref):
    keys = [kc_ref[...], kl_ref[...]]
    vals = [vc_ref[...], vl_ref[...]]
    for e in range(GQA_HEADS // GQA_KV_HEADS):
        sl = slice(e * LANES, (e + 1) * LANES)
        o_ref[:, sl] = _attend(q_ref[:, sl], keys, vals).astype(BF16)


def _gqa_attn(q, kc, vc, kl, vl, batch, seq, ctx_len, tq):
    nq = seq // tq
    return pl.pallas_call(
        _gqa_kernel,
        grid=(batch, GQA_KV_HEADS, nq),
        in_specs=[
            pl.BlockSpec((tq, 256), lambda b, g, t: (b * nq + t, g)),
            pl.BlockSpec((ctx_len, 128), lambda b, g, t: (b, g)),
            pl.BlockSpec((ctx_len, 128), lambda b, g, t: (b, g)),
            pl.BlockSpec((seq, 128), lambda b, g, t: (b, g)),
            pl.BlockSpec((seq, 128), lambda b, g, t: (b, g)),
        ],
        out_specs=pl.BlockSpec((tq, 256), lambda b, g, t: (b * nq + t, g)),
        out_shape=jax.ShapeDtypeStruct((batch * seq, GQA_HEADS * GQA_DIM), BF16),
        compiler_params=_params(),
        name="gqa_attn",
    )(q, kc, vc, kl, vl)


SWA_QB = 128
SWA_GROUP = SWA_HEADS // SWA_KV_HEADS


def _swa_group_lhs(q_ref, r0, rows, g):
    lo = _lane_lo((rows, LANES))
    keep = lo if g == 0 else jnp.logical_not(lo)
    blocks = [jnp.where(keep, q_ref[r0:r0 + rows, j * LANES:(j + 1) * LANES], jnp.zeros((), BF16))
              for j in range(SWA_GROUP)]
    return jnp.concatenate(blocks, axis=0)


def _swa_sink_col(sink_ref, g, rows):
    blk = lax.broadcasted_iota(jnp.int32, (SWA_GROUP * rows, 1), 0) // rows
    col = jnp.full((SWA_GROUP * rows, 1), sink_ref[g * SWA_GROUP], F32)
    for j in range(1, SWA_GROUP):
        col = jnp.where(blk == j, sink_ref[g * SWA_GROUP + j], col)
    return col


def _swa_store(o_ref, r0, rows, o_groups):
    lo = _lane_lo((rows, LANES))
    for j in range(SWA_GROUP):
        a = o_groups[0][j * rows:(j + 1) * rows]
        b = o_groups[1][j * rows:(j + 1) * rows]
        o_ref[r0:r0 + rows, j * LANES:(j + 1) * LANES] = jnp.where(lo, a, b).astype(BF16)


def _swa_lat_kernel(seq, tq, sink_ref, q_ref, kp_ref, kt_ref, kn_ref, vp_ref, vt_ref, vn_ref,
                    kc_ref, vc_ref, o_ref):
    t = pl.program_id(1)
    kband = jnp.concatenate([kp_ref[...], kt_ref[...], kn_ref[...]], axis=0)
    vband = jnp.concatenate([vp_ref[...], vt_ref[...], vn_ref[...]], axis=0)
    band = SWA_QB + 2 * SWA_WINDOW
    rows = SWA_GROUP * SWA_QB
    ii = lax.broadcasted_iota(jnp.int32, (rows, band), 0) % SWA_QB
    jj = lax.broadcasted_iota(jnp.int32, (rows, band), 1)
    rel = jj - SWA_WINDOW - ii
    in_window = jnp.abs(rel) <= SWA_WINDOW
    for i in range(tq // SWA_QB):
        start = t * tq + i * SWA_QB
        kpos = jj + (start - SWA_WINDOW)
        mask = in_window & (kpos >= 0) & (kpos < seq)
        bias = jnp.where(mask, 0.0, NEG_INF).astype(F32)
        kb = kband[i * SWA_QB:i * SWA_QB + band]
        vb = vband[i * SWA_QB:i * SWA_QB + band]
        o_groups = []
        for g in range(SWA_KV_HEADS):
            lhs = _swa_group_lhs(q_ref, i * SWA_QB, SWA_QB, g)
            o_groups.append(_attend(lhs, [kc_ref[...], kb], [vc_ref[...], vb], [None, bias],
                                    _swa_sink_col(sink_ref, g, SWA_QB)))
        _swa_store(o_ref, i * SWA_QB, SWA_QB, o_groups)


def _swa_ctx_kernel(ctx_len, sink_ref, q_ref, kc_ref, vc_ref, o_ref):
    o_groups = []
    for g in range(SWA_KV_HEADS):
        lhs = _swa_group_lhs(q_ref, 0, ctx_len, g)
        o_groups.append(_attend(lhs, [kc_ref[...]], [vc_ref[...]], None, _swa_sink_col(sink_ref, g, ctx_len)))
    _swa_store(o_ref, 0, ctx_len, o_groups)


def _swa_attn_lat(sink, q, k, v, kc, vc, batch, seq, ctx_len, tq):
    nq = seq // tq
    nb = seq // SWA_WINDOW
    per = tq // SWA_WINDOW
    prev_spec = pl.BlockSpec((SWA_WINDOW, 128), lambda b, t: (b * nb + jnp.maximum(t * per - 1, 0), 0))
    next_spec = pl.BlockSpec((SWA_WINDOW, 128), lambda b, t: (b * nb + jnp.minimum(t * per + per, nb - 1), 0))
    tile_spec = pl.BlockSpec((tq, 128), lambda b, t: (b * nq + t, 0))
    ctx_spec = pl.BlockSpec((ctx_len, 128), lambda b, t: (b, 0))
    return pl.pallas_call(
        functools.partial(_swa_lat_kernel, seq, tq),
        grid=(batch, nq),
        in_specs=[
            pl.BlockSpec(memory_space=pltpu.SMEM),
            pl.BlockSpec((tq, 512), lambda b, t: (b * nq + t, 0)),
            prev_spec, tile_spec, next_spec, prev_spec, tile_spec, next_spec,
            ctx_spec, ctx_spec,
        ],
        out_specs=pl.BlockSpec((tq, 512), lambda b, t: (b * nq + t, 0)),
        out_shape=jax.ShapeDtypeStruct((batch * seq, SWA_HEADS * SWA_DIM), BF16),
        compiler_params=_params(),
        name="swa_attn_lat",
    )(sink, q, k, k, k, v, v, v, kc, vc)


def _swa_attn_ctx(sink, q, kc, vc, batch, ctx_len):
    return pl.pallas_call(
        functools.partial(_swa_ctx_kernel, ctx_len),
        grid=(batch,),
        in_specs=[
            pl.BlockSpec(memory_space=pltpu.SMEM),
            pl.BlockSpec((ctx_len, 512), lambda b: (b, 0)),
            pl.BlockSpec((ctx_len, 128), lambda b: (b, 0)),
            pl.BlockSpec((ctx_len, 128), lambda b: (b, 0)),
        ],
        out_specs=pl.BlockSpec((ctx_len, 512), lambda b: (b, 0)),
        out_shape=jax.ShapeDtypeStruct((batch * ctx_len, SWA_HEADS * SWA_DIM), BF16),
        compiler_params=_params(),
        name="swa_attn_ctx",
    )(sink, q, kc, vc)


def _na_kernel(n_tiles, tab_ref, q_ref, k_ref, v_ref, kc_ref, vc_ref, o_ref):
    t = pl.program_id(2)
    grid_rows = n_tiles * NA_TILE_ROWS
    ws = jnp.clip(t * NA_TILE_ROWS - NA_KH // 2, 0, grid_rows - NA_WIN_ROWS)
    variant = jnp.where(t == 0, 0, jnp.where(t == n_tiles - 1, 2, 1))
    start = pl.multiple_of(ws * GRID_W, GRID_W)
    kwin = k_ref[pl.ds(start, NA_WIN_ROWS * GRID_W), :]
    vwin = v_ref[pl.ds(start, NA_WIN_ROWS * GRID_W), :]
    lo = _lane_lo((NA_TILE_ROWS * GRID_W, LANES))
    q = q_ref[...]
    outs = []
    for e in range(2):
        keep = lo if e == 0 else jnp.logical_not(lo)
        qe = jnp.where(keep, q, jnp.zeros((), BF16))
        outs.append(_attend(qe, [kc_ref[...], kwin], [vc_ref[...], vwin], [None, tab_ref[variant, e]]))
    o_ref[...] = jnp.where(lo, outs[0], outs[1]).astype(BF16)


def _na_attn(tab, q, k, v, kc, vc, batch, seq, ctx_len):
    tq = NA_TILE_ROWS * GRID_W
    n_tiles = seq // tq
    nk = NA_WIN_ROWS * GRID_W
    return pl.pallas_call(
        functools.partial(_na_kernel, n_tiles),
        grid=(NA_HEADS // 2, batch, n_tiles),
        in_specs=[
            pl.BlockSpec((3, 2, tq, nk), lambda j, b, t: (0, j, 0, 0)),
            pl.BlockSpec((tq, 128), lambda j, b, t: (b * n_tiles + t, j)),
            pl.BlockSpec((seq, 128), lambda j, b, t: (b, j)),
            pl.BlockSpec((seq, 128), lambda j, b, t: (b, j)),
            pl.BlockSpec((ctx_len, 128), lambda j, b, t: (b, j)),
            pl.BlockSpec((ctx_len, 128), lambda j, b, t: (b, j)),
        ],
        out_specs=pl.BlockSpec((tq, 128), lambda j, b, t: (b * n_tiles + t, j)),
        out_shape=jax.ShapeDtypeStruct((batch * seq, NA_HEADS * NA_DIM), BF16),
        compiler_params=_params(),
        name="na_attn",
    )(tab, q, k, v, kc, vc)


def _na_bias_table(rpb, grid_rows):
    cols = np.arange(GRID_W)
    col_start = np.clip(cols - NA_KW // 2, 0, GRID_W - NA_KW)
    col_ok = (cols[None, :] >= col_start[:, None]) & (cols[None, :] < col_start[:, None] + NA_KW)
    dc = np.clip(cols[None, :] - cols[:, None], -(NA_KW - 1), NA_KW - 1) + NA_KW - 1
    n_tiles = grid_rows // NA_TILE_ROWS
    variants = []
    for t in (0, min(2, n_tiles - 1), n_tiles - 1):
        r0 = t * NA_TILE_ROWS
        ws = int(np.clip(r0 - NA_KH // 2, 0, grid_rows - NA_WIN_ROWS))
        r = r0 + np.arange(NA_TILE_ROWS)
        rs = np.clip(r - NA_KH // 2, 0, grid_rows - NA_KH)
        key_row = ws + np.arange(NA_WIN_ROWS)
        row_ok = (key_row[None, :] >= rs[:, None]) & (key_row[None, :] < rs[:, None] + NA_KH)
        dr = np.clip(key_row[None, :] - r[:, None] + NA_KH - 1, 0, 2 * NA_KH - 2)
        ok = row_ok[:, None, :, None] & col_ok[None, :, None, :]
        bias = rpb[:, dr[:, None, :, None], dc[None, :, None, :]]
        tab = jnp.where(ok[None], bias.astype(F32), NEG_INF)
        variants.append(tab.reshape(rpb.shape[0], NA_TILE_ROWS * GRID_W, NA_WIN_ROWS * GRID_W))
    return jnp.stack(variants, axis=0)


def _post_kernel(a_ref, b_ref, x_ref, mod_ref, wa_ref, wb_ref, g1_ref, g2_ref, x1_ref, h_ref):
    y = _dot(a_ref[...], wa_ref[...]) + _dot(b_ref[...], wb_ref[...])
    x1 = x_ref[...] + mod_ref[0, 2:3, :] * _rms(y, g1_ref[...])
    x1_ref[...] = x1
    h_ref[...] = (_rms(x1, g2_ref[...]) * (1.0 + mod_ref[0, 4:5, :]) + mod_ref[0, 3:4, :]).astype(BF16)


def _post(rows, a, b, x, mod, wa, wb, g1, g2):
    n = rows.n
    return pl.pallas_call(
        _post_kernel,
        grid=rows.grid,
        in_specs=[
            rows.row_spec(a.shape[1]), rows.row_spec(b.shape[1]), rows.row_spec(D_MODEL), rows.mod_spec(),
            _const_spec(wa.shape), _const_spec(wb.shape), _const_spec((1, D_MODEL)), _const_spec((1, D_MODEL)),
        ],
        out_specs=[rows.row_spec(D_MODEL), rows.row_spec(D_MODEL)],
        out_shape=[jax.ShapeDtypeStruct((n, D_MODEL), F32), jax.ShapeDtypeStruct((n, D_MODEL), BF16)],
        compiler_params=_params(),
        name="post",
    )(a, b, x, mod, wa, wb, g1, g2)


def _gelu_tanh(x):
    return x * (0.5 * (1.0 + jnp.tanh(math.sqrt(2.0 / math.pi) * (x + 0.044715 * (x * x * x)))))


def _ffn_kernel(tm, tps, hp_ref, h_ref, hn_ref, x_ref, mod_ref, g_ref, wg_ref, wv_ref, cw_ref, wd_ref,
                o_ref, hh_ref, ug_ref, acc_ref):
    ti = pl.program_id(0) % tps
    hh_ref[0:HALO] = jnp.where(ti == 0, jnp.zeros((), BF16), hp_ref[...])
    hh_ref[HALO:HALO + tm] = h_ref[...]
    hh_ref[HALO + tm:] = jnp.where(ti == tps - 1, jnp.zeros((), BF16), hn_ref[...])
    acc_ref[...] = jnp.zeros_like(acc_ref)

    def chunk(c, carry):
        ug_ref[...] = _dot(hh_ref[...], wg_ref[c])
        val = _dot(h_ref[...], wv_ref[c])
        cw = cw_ref[c]
        gate = cw[3:4]
        for tap in range(3):
            gate = gate + ug_ref[HALO - 1 + tap:HALO - 1 + tap + tm] * cw[tap:tap + 1]
        act = (_gelu_tanh(gate) * val).astype(BF16)
        acc_ref[...] += _dot(act, wd_ref[c])
        return carry

    lax.fori_loop(0, N_FF_CHUNKS, chunk, 0)
    o_ref[...] = x_ref[...] + mod_ref[0, 5:6, :] * _rms(acc_ref[...], g_ref[...])


def _ffn(rows, h, x1, mod, g, wg, wv, cw, wd):
    n, tm, tps = rows.n, rows.tm, rows.tps
    per = tm // HALO
    last = n // HALO - 1
    return pl.pallas_call(
        functools.partial(_ffn_kernel, tm, tps),
        grid=rows.grid,
        in_specs=[
            pl.BlockSpec((HALO, D_MODEL), lambda i: (jnp.maximum(i * per - 1, 0), 0)),
            rows.row_spec(D_MODEL),
            pl.BlockSpec((HALO, D_MODEL), lambda i: (jnp.minimum((i + 1) * per, last), 0)),
            rows.row_spec(D_MODEL), rows.mod_spec(), _const_spec((1, D_MODEL)),
            _const_spec(wg.shape), _const_spec(wv.shape), _const_spec(cw.shape), _const_spec(wd.shape),
        ],
        out_specs=rows.row_spec(D_MODEL),
        out_shape=jax.ShapeDtypeStruct((n, D_MODEL), F32),
        scratch_shapes=[
            pltpu.VMEM((tm + 2 * HALO, D_MODEL), BF16),
            pltpu.VMEM((tm + 2 * HALO, FF_CHUNK), F32),
            pltpu.VMEM((tm, D_MODEL), F32),
        ],
        compiler_params=_params(),
        name="ffn",
    )(h, h, h, x1, mod, g, wg, wv, cw, wd)


def _rope_angles(n, d):
    t = jnp.arange(n)
    rows = (t // GRID_W).astype(F32)
    cols = (t % GRID_W).astype(F32)
    freqs = ROPE_BASE ** (-jnp.arange(d // 4, dtype=F32) * 4.0 / d)
    ang = jnp.concatenate([rows[:, None] * freqs, cols[:, None] * freqs], axis=-1)
    return jnp.cos(ang), jnp.sin(ang)


def _rope_lane_tables(cos, sin, offset, group):
    n, half = cos.shape
    c = jnp.ones((n, group), F32)
    sa = jnp.zeros((n, group), F32)
    sb = jnp.zeros((n, group), F32)
    c = c.at[:, offset:offset + half].set(cos).at[:, offset + half:offset + 2 * half].set(cos)
    sa = sa.at[:, offset:offset + half].set(-sin)
    sb = sb.at[:, offset + half:offset + 2 * half].set(sin)
    reps = LANES // group
    return tuple(jnp.tile(a, (1, reps)) for a in (c, sa, sb))


def _identity_tables(n, width_blocks):
    ones = jnp.ones((n, LANES), F32)
    zeros = jnp.zeros((n, LANES), F32)
    return [ones, zeros, zeros][:width_blocks]


def _even_weights(w_in, w_out, w_uq, w_ukv):
    z = lambda r, c: jnp.zeros((r, c), F32)
    mla_in = MLA_Q_RANK + MLA_KV_RANK + MLA_ROPE
    kpe = w_in[:, MLA_Q_RANK + MLA_KV_RANK:mla_in]
    swa = w_in[:, mla_in:]
    swa_q = swa[:, :SWA_HEADS * SWA_DIM].reshape(D_MODEL, SWA_HEADS, SWA_DIM)
    pair_order = np.array([h for j in range(SWA_GROUP) for h in (j, SWA_GROUP + j)])
    swa_q = swa_q[:, pair_order, :].reshape(D_MODEL, SWA_HEADS * SWA_DIM)
    win = jnp.concatenate([
        w_in[:, :MLA_Q_RANK + MLA_KV_RANK], z(D_MODEL, 64), kpe, z(D_MODEL, 32), swa_q,
        swa[:, SWA_HEADS * SWA_DIM:]], axis=1)
    assert win.shape[1] == EVEN_COLS
    wuq = w_uq.reshape(MLA_Q_RANK, MLA_HEADS, MLA_NOPE + MLA_ROPE)
    wuq = jnp.pad(wuq, ((0, 0), (0, 0), (0, LANES - MLA_NOPE - MLA_ROPE))).reshape(MLA_Q_RANK, MLA_HEADS * LANES)
    wukv = w_ukv.reshape(MLA_KV_RANK, MLA_HEADS, MLA_NOPE + MLA_V)
    wuk = jnp.pad(wukv[:, :, :MLA_NOPE], ((0, 0), (0, 0), (0, LANES - MLA_NOPE))).reshape(MLA_KV_RANK, MLA_HEADS * LANES)
    wuv = wukv[:, :, MLA_NOPE:].reshape(MLA_KV_RANK, MLA_HEADS * MLA_V)
    n_a = MLA_HEADS * MLA_V
    wo_a = w_out[:n_a]
    wo_b = w_out[n_a:].reshape(SWA_HEADS, SWA_DIM, D_MODEL)[pair_order].reshape(SWA_HEADS * SWA_DIM, D_MODEL)
    return tuple(a.astype(BF16) for a in (win, wuq, wuk, wuv, wo_a, wo_b))


def _ffn_weights(w_up, conv_w, conv_b, w_down):
    def chunks(w):
        return w.reshape(D_MODEL, N_FF_CHUNKS, FF_CHUNK).transpose(1, 0, 2).astype(BF16)

    cw = jnp.concatenate([conv_w, conv_b[None], jnp.zeros((4, D_FF), F32)], axis=0)
    cw = cw.reshape(8, N_FF_CHUNKS, FF_CHUNK).transpose(1, 0, 2)
    wd = w_down.reshape(N_FF_CHUNKS, FF_CHUNK, D_MODEL).astype(BF16)
    return chunks(w_up[:, :D_FF]), chunks(w_up[:, D_FF:]), cw, wd


def kernel(x, c, ctx, c_ctx, ada_w, ada_b, pre_mix_g, post_mix_g, pre_ffn_g, post_ffn_g, ffn_up, ffn_conv_w,
           ffn_conv_b, ffn_down, even_w_in, even_w_out, mla_q_norm_g, mla_w_uq, mla_kv_norm_g, mla_w_ukv,
           swa_sink, odd_w_in, odd_w_out, na_rpb, gqa_q_g, gqa_k_g):
    batch, seq, _ = x.shape
    ctx_len = ctx.shape[1]
    depth = ada_w.shape[0]
    n_lat, n_ctx = batch * seq, batch * ctx_len

    pad_rows = -(batch + 1) % 8
    cc = jnp.concatenate([c, c_ctx[None], jnp.zeros((pad_rows, D_MODEL), F32)], axis=0)
    mods = _ada(cc, ada_w, ada_b).reshape(depth, cc.shape[0], 6, D_MODEL)
    mods = jnp.pad(mods, ((0, 0), (0, 0), (0, 2), (0, 0)))

    lat_rows = _Rows(n_lat, seq, 512, True)
    ctx_rows = _Rows(n_ctx, ctx_len, ctx_len, False)

    cos_m, sin_m = _rope_angles(seq, MLA_ROPE)
    cos_s, sin_s = _rope_angles(seq, SWA_DIM)
    cos_g, sin_g = _rope_angles(seq, GQA_DIM)
    gc, gsa, gsb = _rope_lane_tables(cos_g, sin_g, 0, LANES)
    tab_even = jnp.concatenate(_rope_lane_tables(cos_m, sin_m, MLA_NOPE, LANES)
                               + _rope_lane_tables(cos_s, sin_s, 0, SWA_DIM), axis=1)
    tab_odd = jnp.concatenate([gc, gsa + gsb], axis=1)
    tab_even_ctx = jnp.concatenate(_identity_tables(ctx_len, 3) * 2, axis=1)
    tab_odd_ctx = jnp.concatenate(_identity_tables(ctx_len, 2), axis=1)

    x = x.reshape(n_lat, D_MODEL)
    ctx = ctx.reshape(n_ctx, D_MODEL)
    row = lambda v: v.reshape(1, -1)

    for layer in range(depth):
        need_ctx = layer < depth - 1
        i = layer // 2
        mod_lat = mods[layer, :batch]
        mod_ctx = mods[layer, batch:batch + 1]
        if layer % 2 == 0:
            win, wuq, wuk, wuv, wo_a, wo_b = _even_weights(even_w_in[i], even_w_out[i], mla_w_uq[i], mla_w_ukv[i])
            shared = (row(pre_mix_g[layer]), win, row(mla_q_norm_g[i]), wuq, row(mla_kv_norm_g[i]), wuk, wuv)
            qm, km, vm, qs, ks, vs = _pre_even(lat_rows, x, mod_lat, *shared, tab_even)
            qmc, kmc, vmc, qsc, ksc, vsc = _pre_even(ctx_rows, ctx, mod_ctx, *shared, tab_even_ctx)
            a_lat = _mla_attn(qm, kmc, vmc, km, vm, batch, seq, ctx_len, seq, 256)
            b_lat = _swa_attn_lat(swa_sink[i], qs, ks, vs, ksc, vsc, batch, seq, ctx_len, 512)
            if need_ctx:
                a_ctx = _mla_attn(qmc, kmc, vmc, None, None, batch, ctx_len, ctx_len, seq, ctx_len)
                b_ctx = _swa_attn_ctx(swa_sink[i], qsc, ksc, vsc, batch, ctx_len)
        else:
            win = odd_w_in[i].astype(BF16)
            n_a = NA_HEADS * NA_DIM
            wo_a, wo_b = odd_w_out[i][:n_a].astype(BF16), odd_w_out[i][n_a:].astype(BF16)
            shared = (row(pre_mix_g[layer]), win, row(gqa_q_g[i]), row(gqa_k_g[i]))
            qn, kn, vn, qq, kq, vq = _pre_odd(lat_rows, x, mod_lat, *shared, tab_odd)
            qnc, knc, vnc, qqc, kqc, vqc = _pre_odd(ctx_rows, ctx, mod_ctx, *shared, tab_odd_ctx)
            na_tab = _na_bias_table(na_rpb[i], seq // GRID_W)
            a_lat = _na_attn(na_tab, qn, kn, vn, knc, vnc, batch, seq, ctx_len)
            b_lat = _gqa_attn(qq, kqc, vqc, kq, vq, batch, seq, ctx_len, 256)
            assert not need_ctx, "context-query attention for odd layers is only needed when depth > 2"
        wg, wv, cw, wd = _ffn_weights(ffn_up[layer], ffn_conv_w[layer], ffn_conv_b[layer], ffn_down[layer])
        g1, g2, g3 = row(post_mix_g[layer]), row(pre_ffn_g[layer]), row(post_ffn_g[layer])
        x1, h = _post(lat_rows, a_lat, b_lat, x, mod_lat, wo_a, wo_b, g1, g2)
        x = _ffn(lat_rows, h, x1, mod_lat, g3, wg, wv, cw, wd)
        if need_ctx:
            c1, hc = _post(ctx_rows, a_ctx, b_ctx, ctx, mod_ctx, wo_a, wo_b, g1, g2)
            ctx = _ffn(ctx_rows, hc, c1, mod_ctx, g3, wg, wv, cw, wd)
    return x.reshape(batch, seq, D_MODEL)
```

```python
import functools
import math

import jax
import jax.numpy as jnp
import numpy as np
from jax import lax
from jax.experimental import pallas as pl
from jax.experimental.pallas import tpu as pltpu

F32 = jnp.float32
BF16 = jnp.bfloat16

D_MODEL = 1024
GRID_W = 64
ROPE_BASE = 10000.0
EPS = 1e-6
NEG_INF = -1e30
LANES = 128

MLA_HEADS, MLA_Q_RANK, MLA_KV_RANK, MLA_NOPE, MLA_ROPE, MLA_V = 8, 256, 128, 64, 32, 64
SWA_HEADS, SWA_KV_HEADS, SWA_DIM, SWA_WINDOW = 8, 2, 64, 128
NA_HEADS, NA_DIM, NA_KH, NA_KW = 8, 64, 8, 16
GQA_HEADS, GQA_KV_HEADS, GQA_DIM = 4, 2, 128
D_FF = 2816
FF_CHUNK = 256
N_FF_CHUNKS = D_FF // FF_CHUNK
HALO = 16

NA_TILE_ROWS = 4
NA_WIN_ROWS = NA_TILE_ROWS + NA_KH - 1

VMEM_LIMIT = 56 * 1024 * 1024


def _params():
    return pltpu.CompilerParams(vmem_limit_bytes=VMEM_LIMIT)


def _dot(a, b):
    return jnp.dot(a, b, preferred_element_type=F32)


def _dot_nt(a, b):
    return lax.dot_general(a, b, (((1,), (1,)), ((), ())), preferred_element_type=F32)


def _rms(x, g):
    return x * lax.rsqrt(jnp.mean(x * x, axis=-1, keepdims=True) + EPS) * g


def _rope(x, c, sa, sb, sha, shb):
    return x * c + pltpu.roll(x, sha, axis=1) * sa + pltpu.roll(x, shb, axis=1) * sb


def _lane_lo(shape):
    return lax.broadcasted_iota(jnp.int32, shape, len(shape) - 1) < (LANES // 2)


def _const_spec(shape):
    nd = len(shape)
    return pl.BlockSpec(shape, lambda *_: (0,) * nd)


def _ada_kernel(cc_ref, w_ref, b_ref, o_ref):
    cc = cc_ref[...]
    s = cc * (1.0 / (1.0 + jnp.exp(-cc)))
    o_ref[0] = _dot(s.astype(BF16), w_ref[0].astype(BF16)) + b_ref[0]


def _ada(cc, ada_w, ada_b):
    depth = ada_w.shape[0]
    rows = cc.shape[0]
    return pl.pallas_call(
        _ada_kernel,
        grid=(depth, 6),
        in_specs=[
            pl.BlockSpec((rows, D_MODEL), lambda l, j: (0, 0)),
            pl.BlockSpec((1, D_MODEL, D_MODEL), lambda l, j: (l, 0, j)),
            pl.BlockSpec((1, 1, D_MODEL), lambda l, j: (l, 0, j)),
        ],
        out_specs=pl.BlockSpec((1, rows, D_MODEL), lambda l, j: (l, 0, j)),
        out_shape=jax.ShapeDtypeStruct((depth, rows, 6 * D_MODEL), F32),
        compiler_params=_params(),
        name="ada",
    )(cc, ada_w, ada_b.reshape(depth, 1, 6 * D_MODEL))


class _Rows:
    def __init__(self, n, seq, tm, per_batch_mod):
        assert seq % tm == 0 and n % seq == 0
        self.n, self.seq, self.tm = n, seq, tm
        self.tps = seq // tm
        self.grid = (n // tm,)
        self.per_batch_mod = per_batch_mod

    def row_spec(self, width):
        return pl.BlockSpec((self.tm, width), lambda i: (i, 0))

    def mod_spec(self):
        if self.per_batch_mod:
            tps = self.tps
            return pl.BlockSpec((1, 8, D_MODEL), lambda i: (i // tps, 0, 0))
        return pl.BlockSpec((1, 8, D_MODEL), lambda i: (0, 0, 0))

    def table_spec(self, width):
        tps = self.tps
        return pl.BlockSpec((self.tm, width), lambda i: (i % tps, 0))


def _modulated(x_ref, mod_ref, g_ref, shift_row, scale_row):
    x = x_ref[...]
    shift = mod_ref[0, shift_row:shift_row + 1, :]
    scale = mod_ref[0, scale_row:scale_row + 1, :]
    return _rms(x, g_ref[...]) * (1.0 + scale) + shift


EVEN_COLS = 1280


def _pre_even_kernel(x_ref, mod_ref, g_ref, win_ref, qg_ref, wuq_ref, kvg_ref, wuk_ref, wuv_ref,
                     tab_ref, qm_ref, km_ref, vm_ref, qs_ref, ks_ref, vs_ref):
    h = _modulated(x_ref, mod_ref, g_ref, 0, 1).astype(BF16)
    p = _dot(h, win_ref[...])
    mc, ma, mb = tab_ref[:, 0:128], tab_ref[:, 128:256], tab_ref[:, 256:384]
    sc, sa, sb = tab_ref[:, 384:512], tab_ref[:, 512:640], tab_ref[:, 640:768]

    cq = _rms(p[:, 0:256], qg_ref[...]).astype(BF16)
    q = _dot(cq, wuq_ref[...])
    mla_scale = (MLA_NOPE + MLA_ROPE) ** -0.5
    ckv = _rms(p[:, 256:384], kvg_ref[...]).astype(BF16)
    kn = _dot(ckv, wuk_ref[...])
    kpe = _rope(p[:, 384:512], mc, ma, mb, 112, 16)
    for j in range(MLA_HEADS):
        sl = slice(j * LANES, (j + 1) * LANES)
        qm_ref[:, sl] = (_rope(q[:, sl], mc, ma, mb, 112, 16) * mla_scale).astype(BF16)
        km_ref[:, sl] = (kn[:, sl] + kpe).astype(BF16)
    vm_ref[...] = _dot(ckv, wuv_ref[...]).astype(BF16)

    swa_scale = SWA_DIM ** -0.5
    for j in range(4):
        sl = slice(512 + j * LANES, 512 + (j + 1) * LANES)
        qs_ref[:, j * LANES:(j + 1) * LANES] = (_rope(p[:, sl], sc, sa, sb, 96, 32) * swa_scale).astype(BF16)
    ks_ref[...] = _rope(p[:, 1024:1152], sc, sa, sb, 96, 32).astype(BF16)
    vs_ref[...] = p[:, 1152:1280].astype(BF16)


def _pre_even(rows, x, mod, g, win, qg, wuq, kvg, wuk, wuv, tab):
    n = rows.n
    widths = (1024, 1024, 512, 512, 128, 128)
    return pl.pallas_call(
        _pre_even_kernel,
        grid=rows.grid,
        in_specs=[
            rows.row_spec(D_MODEL), rows.mod_spec(), _const_spec((1, D_MODEL)),
            _const_spec(win.shape), _const_spec(qg.shape), _const_spec(wuq.shape),
            _const_spec(kvg.shape), _const_spec(wuk.shape), _const_spec(wuv.shape),
            rows.table_spec(768),
        ],
        out_specs=[rows.row_spec(w) for w in widths],
        out_shape=[jax.ShapeDtypeStruct((n, w), BF16) for w in widths],
        compiler_params=_params(),
        name="pre_even",
    )(x, mod, g, win, qg, wuq, kvg, wuk, wuv, tab)


def _pre_odd_kernel(x_ref, mod_ref, g_ref, win_ref, qg_ref, kg_ref, tab_ref,
                    qn_ref, kn_ref, vn_ref, qq_ref, kq_ref, vq_ref):
    h = _modulated(x_ref, mod_ref, g_ref, 0, 1).astype(BF16)
    p = _dot(h, win_ref[...])
    gc, gs = tab_ref[:, 0:128], tab_ref[:, 128:256]
    na_scale = NA_DIM ** -0.5
    qn_ref[...] = (p[:, 0:512] * na_scale).astype(BF16)
    kn_ref[...] = p[:, 512:1024].astype(BF16)
    vn_ref[...] = p[:, 1024:1536].astype(BF16)
    gqa_scale = GQA_DIM ** -0.5

    def normed(blk, gain):
        y = _rms(blk, gain)
        return y * gc + pltpu.roll(y, 64, axis=1) * gs

    for j in range(GQA_HEADS):
        sl = slice(1536 + j * LANES, 1536 + (j + 1) * LANES)
        qq_ref[:, j * LANES:(j + 1) * LANES] = (normed(p[:, sl], qg_ref[...]) * gqa_scale).astype(BF16)
    for j in range(GQA_KV_HEADS):
        sl = slice(2048 + j * LANES, 2048 + (j + 1) * LANES)
        kq_ref[:, j * LANES:(j + 1) * LANES] = normed(p[:, sl], kg_ref[...]).astype(BF16)
    vq_ref[...] = p[:, 2304:2560].astype(BF16)


def _pre_odd(rows, x, mod, g, win, qg, kg, tab):
    n = rows.n
    widths = (512, 512, 512, 512, 256, 256)
    return pl.pallas_call(
        _pre_odd_kernel,
        grid=rows.grid,
        in_specs=[
            rows.row_spec(D_MODEL), rows.mod_spec(), _const_spec((1, D_MODEL)),
            _const_spec(win.shape), _const_spec(qg.shape), _const_spec(kg.shape),
            rows.table_spec(256),
        ],
        out_specs=[rows.row_spec(w) for w in widths],
        out_shape=[jax.ShapeDtypeStruct((n, w), BF16) for w in widths],
        compiler_params=_params(),
        name="pre_odd",
    )(x, mod, g, win, qg, kg, tab)


def _attend(q, keys, values, biases=None, sink=None):
    scores = [_dot_nt(q, k) for k in keys]
    if biases is not None:
        scores = [s if b is None else s + b for s, b in zip(scores, biases)]
    m = functools.reduce(jnp.maximum, [jnp.max(s, axis=-1, keepdims=True) for s in scores])
    if sink is not None:
        m = jnp.maximum(m, sink)
    probs = [jnp.exp(s - m) for s in scores]
    l = functools.reduce(jnp.add, [jnp.sum(p, axis=-1, keepdims=True) for p in probs])
    if sink is not None:
        l = l + jnp.exp(sink - m)
    o = functools.reduce(jnp.add, [_dot(p.astype(BF16), v) for p, v in zip(probs, values)])
    return o / l


def _mla_kernel(has_lat, q_ref, kc_ref, vc_ref, *rest):
    if has_lat:
        kl_ref, vl_ref, o_ref = rest
    else:
        (o_ref,) = rest
    outs = []
    for e in range(2):
        sl = slice(e * LANES, (e + 1) * LANES)
        keys, vals = [kc_ref[:, sl]], [vc_ref[...]]
        if has_lat:
            keys.append(kl_ref[:, sl])
            vals.append(vl_ref[...])
        outs.append(_attend(q_ref[:, sl], keys, vals))
    o_ref[...] = jnp.where(_lane_lo(outs[0].shape), outs[0], outs[1]).astype(BF16)


def _mla_attn(q, kc, vc, kl, vl, batch, q_seq, ctx_len, lat_seq, tq):
    has_lat = kl is not None
    nq = q_seq // tq
    in_specs = [
        pl.BlockSpec((tq, 256), lambda b, j, t: (b * nq + t, j)),
        pl.BlockSpec((ctx_len, 256), lambda b, j, t: (b, j)),
        pl.BlockSpec((ctx_len, 128), lambda b, j, t: (b, j)),
    ]
    args = [q, kc, vc]
    if has_lat:
        in_specs += [
            pl.BlockSpec((lat_seq, 256), lambda b, j, t: (b, j)),
            pl.BlockSpec((lat_seq, 128), lambda b, j, t: (b, j)),
        ]
        args += [kl, vl]
    return pl.pallas_call(
        functools.partial(_mla_kernel, has_lat),
        grid=(batch, MLA_HEADS // 2, nq),
        in_specs=in_specs,
        out_specs=pl.BlockSpec((tq, 128), lambda b, j, t: (b * nq + t, j)),
        out_shape=jax.ShapeDtypeStruct((batch * q_seq, MLA_HEADS * MLA_V), BF16),
        compiler_params=_params(),
        name="mla_attn_lat" if has_lat else "mla_attn_ctx",
    )(*args)


def _gqa_kernel(q_ref, kc_ref, vc_ref, kl_ref, vl_ref, o_ref):
    keys = [kc_ref[...], kl_ref[...]]
    vals = [vc_ref[...], vl_ref[...]]
    for e in range(GQA_HEADS // GQA_KV_HEADS):
        sl = slice(e * LANES, (e + 1) * LANES)
        o_ref[:, sl] = _attend(q_ref[:, sl], keys, vals).astype(BF16)


def _gqa_attn(q, kc, vc, kl, vl, batch, seq, ctx_len, tq):
    nq = seq // tq
    return pl.pallas_call(
        _gqa_kernel,
        grid=(batch, GQA_KV_HEADS, nq),
        in_specs=[
            pl.BlockSpec((tq, 256), lambda b, g, t: (b * nq + t, g)),
            pl.BlockSpec((ctx_len, 128), lambda b, g, t: (b, g)),
            pl.BlockSpec((ctx_len, 128), lambda b, g, t: (b, g)),
            pl.BlockSpec((seq, 128), lambda b, g, t: (b, g)),
            pl.BlockSpec((seq, 128), lambda b, g, t: (b, g)),
        ],
        out_specs=pl.BlockSpec((tq, 256), lambda b, g, t: (b * nq + t, g)),
        out_shape=jax.ShapeDtypeStruct((batch * seq, GQA_HEADS * GQA_DIM), BF16),
        compiler_params=_params(),
        name="gqa_attn",
    )(q, kc, vc, kl, vl)


SWA_QB = 128
SWA_GROUP = SWA_HEADS // SWA_KV_HEADS


def _swa_group_lhs(q_ref, r0, rows, g):
    lo = _lane_lo((rows, LANES))
    keep = lo if g == 0 else jnp.logical_not(lo)
    blocks = [jnp.where(keep, q_ref[r0:r0 + rows, j * LANES:(j + 1) * LANES], jnp.zeros((), BF16))
              for j in range(SWA_GROUP)]
    return jnp.concatenate(blocks, axis=0)


def _swa_sink_col(sink_ref, g, rows):
    blk = lax.broadcasted_iota(jnp.int32, (SWA_GROUP * rows, 1), 0) // rows
    col = jnp.full((SWA_GROUP * rows, 1), sink_ref[g * SWA_GROUP], F32)
    for j in range(1, SWA_GROUP):
        col = jnp.where(blk == j, sink_ref[g * SWA_GROUP + j], col)
    return col


def _swa_store(o_ref, r0, rows, o_groups):
    lo = _lane_lo((rows, LANES))
    for j in range(SWA_GROUP):
        a = o_groups[0][j * rows:(j + 1) * rows]
        b = o_groups[1][j * rows:(j + 1) * rows]
        o_ref[r0:r0 + rows, j * LANES:(j + 1) * LANES] = jnp.where(lo, a, b).astype(BF16)


def _swa_lat_kernel(seq, tq, sink_ref, q_ref, kp_ref, kt_ref, kn_ref, vp_ref, vt_ref, vn_ref,
                    kc_ref, vc_ref, o_ref):
    t = pl.program_id(1)
    kband = jnp.concatenate([kp_ref[...], kt_ref[...], kn_ref[...]], axis=0)
    vband = jnp.concatenate([vp_ref[...], vt_ref[...], vn_ref[...]], axis=0)
    band = SWA_QB + 2 * SWA_WINDOW
    rows = SWA_GROUP * SWA_QB
    ii = lax.broadcasted_iota(jnp.int32, (rows, band), 0) % SWA_QB
    jj = lax.broadcasted_iota(jnp.int32, (rows, band), 1)
    rel = jj - SWA_WINDOW - ii
    in_window = jnp.abs(rel) <= SWA_WINDOW
    for i in range(tq // SWA_QB):
        start = t * tq + i * SWA_QB
        kpos = jj + (start - SWA_WINDOW)
        mask = in_window & (kpos >= 0) & (kpos < seq)
        bias = jnp.where(mask, 0.0, NEG_INF).astype(F32)
        kb = kband[i * SWA_QB:i * SWA_QB + band]
        vb = vband[i * SWA_QB:i * SWA_QB + band]
        o_groups = []
        for g in range(SWA_KV_HEADS):
            lhs = _swa_group_lhs(q_ref, i * SWA_QB, SWA_QB, g)
            o_groups.append(_attend(lhs, [kc_ref[...], kb], [vc_ref[...], vb], [None, bias],
                                    _swa_sink_col(sink_ref, g, SWA_QB)))
        _swa_store(o_ref, i * SWA_QB, SWA_QB, o_groups)


def _swa_ctx_kernel(ctx_len, sink_ref, q_ref, kc_ref, vc_ref, o_ref):
    o_groups = []
    for g in range(SWA_KV_HEADS):
        lhs = _swa_group_lhs(q_ref, 0, ctx_len, g)
        o_groups.append(_attend(lhs, [kc_ref[...]], [vc_ref[...]], None, _swa_sink_col(sink_ref, g, ctx_len)))
    _swa_store(o_ref, 0, ctx_len, o_groups)


def _swa_attn_lat(sink, q, k, v, kc, vc, batch, seq, ctx_len, tq):
    nq = seq // tq
    nb = seq // SWA_WINDOW
    per = tq // SWA_WINDOW
    prev_spec = pl.BlockSpec((SWA_WINDOW, 128), lambda b, t: (b * nb + jnp.maximum(t * per - 1, 0), 0))
    next_spec = pl.BlockSpec((SWA_WINDOW, 128), lambda b, t: (b * nb + jnp.minimum(t * per + per, nb - 1), 0))
    tile_spec = pl.BlockSpec((tq, 128), lambda b, t: (b * nq + t, 0))
    ctx_spec = pl.BlockSpec((ctx_len, 128), lambda b, t: (b, 0))
    return pl.pallas_call(
        functools.partial(_swa_lat_kernel, seq, tq),
        grid=(batch, nq),
        in_specs=[
            pl.BlockSpec(memory_space=pltpu.SMEM),
            pl.BlockSpec((tq, 512), lambda b, t: (b * nq + t, 0)),
            prev_spec, tile_spec, next_spec, prev_spec, tile_spec, next_spec,
            ctx_spec, ctx_spec,
        ],
        out_specs=pl.BlockSpec((tq, 512), lambda b, t: (b * nq + t, 0)),
        out_shape=jax.ShapeDtypeStruct((batch * seq, SWA_HEADS * SWA_DIM), BF16),
        compiler_params=_params(),
        name="swa_attn_lat",
    )(sink, q, k, k, k, v, v, v, kc, vc)


def _swa_attn_ctx(sink, q, kc, vc, batch, ctx_len):
    return pl.pallas_call(
        functools.partial(_swa_ctx_kernel, ctx_len),
        grid=(batch,),
        in_specs=[
            pl.BlockSpec(memory_space=pltpu.SMEM),
            pl.BlockSpec((ctx_len, 512), lambda b: (b, 0)),
            pl.BlockSpec((ctx_len, 128), lambda b: (b, 0)),
            pl.BlockSpec((ctx_len, 128), lambda b: (b, 0)),
        ],
        out_specs=pl.BlockSpec((ctx_len, 512), lambda b: (b, 0)),
        out_shape=jax.ShapeDtypeStruct((batch * ctx_len, SWA_HEADS * SWA_DIM), BF16),
        compiler_params=_params(),
        name="swa_attn_ctx",
    )(sink, q, kc, vc)


def _na_kernel(n_tiles, tab_ref, q_ref, k_ref, v_ref, kc_ref, vc_ref, o_ref):
    t = pl.program_id(2)
    grid_rows = n_tiles * NA_TILE_ROWS
    ws = jnp.clip(t * NA_TILE_ROWS - NA_KH // 2, 0, grid_rows - NA_WIN_ROWS)
    variant = jnp.where(t == 0, 0, jnp.where(t == n_tiles - 1, 2, 1))
    start = pl.multiple_of(ws * GRID_W, GRID_W)
    kwin = k_ref[pl.ds(start, NA_WIN_ROWS * GRID_W), :]
    vwin = v_ref[pl.ds(start, NA_WIN_ROWS * GRID_W), :]
    lo = _lane_lo((NA_TILE_ROWS * GRID_W, LANES))
    q = q_ref[...]
    outs = []
    for e in range(2):
        keep = lo if e == 0 else jnp.logical_not(lo)
        qe = jnp.where(keep, q, jnp.zeros((), BF16))
        outs.append(_attend(qe, [kc_ref[...], kwin], [vc_ref[...], vwin], [None, tab_ref[variant, e]]))
    o_ref[...] = jnp.where(lo, outs[0], outs[1]).astype(BF16)


def _na_attn(tab, q, k, v, kc, vc, batch, seq, ctx_len):
    tq = NA_TILE_ROWS * GRID_W
    n_tiles = seq // tq
    nk = NA_WIN_ROWS * GRID_W
    return pl.pallas_call(
        functools.partial(_na_kernel, n_tiles),
        grid=(NA_HEADS // 2, batch, n_tiles),
        in_specs=[
            pl.BlockSpec((3, 2, tq, nk), lambda j, b, t: (0, j, 0, 0)),
            pl.BlockSpec((tq, 128), lambda j, b, t: (b * n_tiles + t, j)),
            pl.BlockSpec((seq, 128), lambda j, b, t: (b, j)),
            pl.BlockSpec((seq, 128), lambda j, b, t: (b, j)),
            pl.BlockSpec((ctx_len, 128), lambda j, b, t: (b, j)),
            pl.BlockSpec((ctx_len, 128), lambda j, b, t: (b, j)),
        ],
        out_specs=pl.BlockSpec((tq, 128), lambda j, b, t: (b * n_tiles + t, j)),
        out_shape=jax.ShapeDtypeStruct((batch * seq, NA_HEADS * NA_DIM), BF16),
        compiler_params=_params(),
        name="na_attn",
    )(tab, q, k, v, kc, vc)


def _na_bias_table(rpb, grid_rows):
    cols = np.arange(GRID_W)
    col_start = np.clip(cols - NA_KW // 2, 0, GRID_W - NA_KW)
    col_ok = (cols[None, :] >= col_start[:, None]) & (cols[None, :] < col_start[:, None] + NA_KW)
    dc = np.clip(cols[None, :] - cols[:, None], -(NA_KW - 1), NA_KW - 1) + NA_KW - 1
    n_heads, n_dr = rpb.shape[0], rpb.shape[1]
    by_col = jnp.take(rpb.astype(F32), jnp.asarray(dc.reshape(-1)), axis=2).reshape(n_heads, n_dr, GRID_W, GRID_W)
    by_col = jnp.where(jnp.asarray(col_ok)[None, None], by_col, NEG_INF)
    by_col = jnp.concatenate([by_col, jnp.full((n_heads, 1, GRID_W, GRID_W), NEG_INF, F32)], axis=1)
    n_tiles = grid_rows // NA_TILE_ROWS
    variants = []
    for t in (0, min(2, n_tiles - 1), n_tiles - 1):
        r0 = t * NA_TILE_ROWS
        ws = int(np.clip(r0 - NA_KH // 2, 0, grid_rows - NA_WIN_ROWS))
        r = r0 + np.arange(NA_TILE_ROWS)
        rs = np.clip(r - NA_KH // 2, 0, grid_rows - NA_KH)
        key_row = ws + np.arange(NA_WIN_ROWS)
        row_ok = (key_row[None, :] >= rs[:, None]) & (key_row[None, :] < rs[:, None] + NA_KH)
        dr = np.where(row_ok, key_row[None, :] - r[:, None] + NA_KH - 1, n_dr)
        rows = []
        for i in range(NA_TILE_ROWS):
            rows.append(jnp.concatenate([by_col[:, int(d)] for d in dr[i]], axis=-1))
        variants.append(jnp.concatenate(rows, axis=1))
    return jnp.stack(variants, axis=0)


def _post_kernel(a_ref, b_ref, x_ref, mod_ref, wa_ref, wb_ref, g1_ref, g2_ref, x1_ref, h_ref):
    y = _dot(a_ref[...], wa_ref[...]) + _dot(b_ref[...], wb_ref[...])
    x1 = x_ref[...] + mod_ref[0, 2:3, :] * _rms(y, g1_ref[...])
    x1_ref[...] = x1
    h_ref[...] = (_rms(x1, g2_ref[...]) * (1.0 + mod_ref[0, 4:5, :]) + mod_ref[0, 3:4, :]).astype(BF16)


def _post(rows, a, b, x, mod, wa, wb, g1, g2):
    n = rows.n
    return pl.pallas_call(
        _post_kernel,
        grid=rows.grid,
        in_specs=[
            rows.row_spec(a.shape[1]), rows.row_spec(b.shape[1]), rows.row_spec(D_MODEL), rows.mod_spec(),
            _const_spec(wa.shape), _const_spec(wb.shape), _const_spec((1, D_MODEL)), _const_spec((1, D_MODEL)),
        ],
        out_specs=[rows.row_spec(D_MODEL), rows.row_spec(D_MODEL)],
        out_shape=[jax.ShapeDtypeStruct((n, D_MODEL), F32), jax.ShapeDtypeStruct((n, D_MODEL), BF16)],
        compiler_params=_params(),
        name="post",
    )(a, b, x, mod, wa, wb, g1, g2)


def _gelu_tanh(x):
    return x * (0.5 * (1.0 + jnp.tanh(math.sqrt(2.0 / math.pi) * (x + 0.044715 * (x * x * x)))))


def _ffn_kernel(tm, tps, hp_ref, h_ref, hn_ref, x_ref, mod_ref, g_ref, wg_ref, wv_ref, cw_ref, wd_ref,
                o_ref, hh_ref, ug_ref, uv_ref, acc_ref):
    ti = pl.program_id(0) % tps
    hh_ref[0:HALO] = jnp.where(ti == 0, jnp.zeros((), BF16), hp_ref[...])
    hh_ref[HALO:HALO + tm] = h_ref[...]
    hh_ref[HALO + tm:] = jnp.where(ti == tps - 1, jnp.zeros((), BF16), hn_ref[...])
    acc_ref[...] = jnp.zeros_like(acc_ref)

    def up(c, slot):
        ug_ref[slot] = _dot(hh_ref[...], wg_ref[c])
        uv_ref[slot] = _dot(h_ref[...], wv_ref[c])

    def down(c, slot):
        cw = cw_ref[c]
        gate = cw[3:4]
        for tap in range(3):
            gate = gate + ug_ref[slot, HALO - 1 + tap:HALO - 1 + tap + tm] * cw[tap:tap + 1]
        act = (_gelu_tanh(gate) * uv_ref[slot]).astype(BF16)
        acc_ref[...] += _dot(act, wd_ref[c])

    up(0, 0)

    def step(i, carry):
        c = 2 * i
        up(c + 1, 1)
        down(c, 0)
        up(c + 2, 0)
        down(c + 1, 1)
        return carry

    assert N_FF_CHUNKS % 2 == 1
    lax.fori_loop(0, N_FF_CHUNKS // 2, step, 0)
    down(N_FF_CHUNKS - 1, 0)
    o_ref[...] = x_ref[...] + mod_ref[0, 5:6, :] * _rms(acc_ref[...], g_ref[...])


def _ffn(rows, h, x1, mod, g, wg, wv, cw, wd):
    n, tm, tps = rows.n, rows.tm, rows.tps
    per = tm // HALO
    last = n // HALO - 1
    return pl.pallas_call(
        functools.partial(_ffn_kernel, tm, tps),
        grid=rows.grid,
        in_specs=[
            pl.BlockSpec((HALO, D_MODEL), lambda i: (jnp.maximum(i * per - 1, 0), 0)),
            rows.row_spec(D_MODEL),
            pl.BlockSpec((HALO, D_MODEL), lambda i: (jnp.minimum((i + 1) * per, last), 0)),
            rows.row_spec(D_MODEL), rows.mod_spec(), _const_spec((1, D_MODEL)),
            _const_spec(wg.shape), _const_spec(wv.shape), _const_spec(cw.shape), _const_spec(wd.shape),
        ],
        out_specs=rows.row_spec(D_MODEL),
        out_shape=jax.ShapeDtypeStruct((n, D_MODEL), F32),
        scratch_shapes=[
            pltpu.VMEM((tm + 2 * HALO, D_MODEL), BF16),
            pltpu.VMEM((2, tm + 2 * HALO, FF_CHUNK), F32),
            pltpu.VMEM((2, tm, FF_CHUNK), F32),
            pltpu.VMEM((tm, D_MODEL), F32),
        ],
        compiler_params=_params(),
        name="ffn",
    )(h, h, h, x1, mod, g, wg, wv, cw, wd)


def _rope_angles(n, d):
    t = jnp.arange(n)
    rows = (t // GRID_W).astype(F32)
    cols = (t % GRID_W).astype(F32)
    freqs = ROPE_BASE ** (-jnp.arange(d // 4, dtype=F32) * 4.0 / d)
    ang = jnp.concatenate([rows[:, None] * freqs, cols[:, None] * freqs], axis=-1)
    return jnp.cos(ang), jnp.sin(ang)


def _rope_lane_tables(cos, sin, offset, group):
    n, half = cos.shape
    c = jnp.ones((n, group), F32)
    sa = jnp.zeros((n, group), F32)
    sb = jnp.zeros((n, group), F32)
    c = c.at[:, offset:offset + half].set(cos).at[:, offset + half:offset + 2 * half].set(cos)
    sa = sa.at[:, offset:offset + half].set(-sin)
    sb = sb.at[:, offset + half:offset + 2 * half].set(sin)
    reps = LANES // group
    return tuple(jnp.tile(a, (1, reps)) for a in (c, sa, sb))


def _identity_tables(n, width_blocks):
    ones = jnp.ones((n, LANES), F32)
    zeros = jnp.zeros((n, LANES), F32)
    return [ones, zeros, zeros][:width_blocks]


def _even_weights(w_in, w_out, w_uq, w_ukv):
    z = lambda r, c: jnp.zeros((r, c), F32)
    mla_in = MLA_Q_RANK + MLA_KV_RANK + MLA_ROPE
    kpe = w_in[:, MLA_Q_RANK + MLA_KV_RANK:mla_in]
    swa = w_in[:, mla_in:]
    swa_q = swa[:, :SWA_HEADS * SWA_DIM].reshape(D_MODEL, SWA_HEADS, SWA_DIM)
    pair_order = np.array([h for j in range(SWA_GROUP) for h in (j, SWA_GROUP + j)])
    swa_q = swa_q[:, pair_order, :].reshape(D_MODEL, SWA_HEADS * SWA_DIM)
    win = jnp.concatenate([
        w_in[:, :MLA_Q_RANK + MLA_KV_RANK], z(D_MODEL, 64), kpe, z(D_MODEL, 32), swa_q,
        swa[:, SWA_HEADS * SWA_DIM:]], axis=1)
    assert win.shape[1] == EVEN_COLS
    wuq = w_uq.reshape(MLA_Q_RANK, MLA_HEADS, MLA_NOPE + MLA_ROPE)
    wuq = jnp.pad(wuq, ((0, 0), (0, 0), (0, LANES - MLA_NOPE - MLA_ROPE))).reshape(MLA_Q_RANK, MLA_HEADS * LANES)
    wukv = w_ukv.reshape(MLA_KV_RANK, MLA_HEADS, MLA_NOPE + MLA_V)
    wuk = jnp.pad(wukv[:, :, :MLA_NOPE], ((0, 0), (0, 0), (0, LANES - MLA_NOPE))).reshape(MLA_KV_RANK, MLA_HEADS * LANES)
    wuv = wukv[:, :, MLA_NOPE:].reshape(MLA_KV_RANK, MLA_HEADS * MLA_V)
    n_a = MLA_HEADS * MLA_V
    wo_a = w_out[:n_a]
    wo_b = w_out[n_a:].reshape(SWA_HEADS, SWA_DIM, D_MODEL)[pair_order].reshape(SWA_HEADS * SWA_DIM, D_MODEL)
    return tuple(a.astype(BF16) for a in (win, wuq, wuk, wuv, wo_a, wo_b))


def _ffn_weights(w_up, conv_w, conv_b, w_down):
    def chunks(w):
        return w.reshape(D_MODEL, N_FF_CHUNKS, FF_CHUNK).transpose(1, 0, 2).astype(BF16)

    cw = jnp.concatenate([conv_w, conv_b[None], jnp.zeros((4, D_FF), F32)], axis=0)
    cw = cw.reshape(8, N_FF_CHUNKS, FF_CHUNK).transpose(1, 0, 2)
    wd = w_down.reshape(N_FF_CHUNKS, FF_CHUNK, D_MODEL).astype(BF16)
    return chunks(w_up[:, :D_FF]), chunks(w_up[:, D_FF:]), cw, wd


def kernel(x, c, ctx, c_ctx, ada_w, ada_b, pre_mix_g, post_mix_g, pre_ffn_g, post_ffn_g, ffn_up, ffn_conv_w,
           ffn_conv_b, ffn_down, even_w_in, even_w_out, mla_q_norm_g, mla_w_uq, mla_kv_norm_g, mla_w_ukv,
           swa_sink, odd_w_in, odd_w_out, na_rpb, gqa_q_g, gqa_k_g):
    batch, seq, _ = x.shape
    ctx_len = ctx.shape[1]
    depth = ada_w.shape[0]
    n_lat, n_ctx = batch * seq, batch * ctx_len

    pad_rows = -(batch + 1) % 8
    cc = jnp.concatenate([c, c_ctx[None], jnp.zeros((pad_rows, D_MODEL), F32)], axis=0)
    mods = _ada(cc, ada_w, ada_b).reshape(depth, cc.shape[0], 6, D_MODEL)
    mods = jnp.pad(mods, ((0, 0), (0, 0), (0, 2), (0, 0)))

    lat_rows = _Rows(n_lat, seq, 512, True)
    ctx_rows = _Rows(n_ctx, ctx_len, ctx_len, False)

    cos_m, sin_m = _rope_angles(seq, MLA_ROPE)
    cos_s, sin_s = _rope_angles(seq, SWA_DIM)
    cos_g, sin_g = _rope_angles(seq, GQA_DIM)
    gc, gsa, gsb = _rope_lane_tables(cos_g, sin_g, 0, LANES)
    tab_even = jnp.concatenate(_rope_lane_tables(cos_m, sin_m, MLA_NOPE, LANES)
                               + _rope_lane_tables(cos_s, sin_s, 0, SWA_DIM), axis=1)
    tab_odd = jnp.concatenate([gc, gsa + gsb], axis=1)
    tab_even_ctx = jnp.concatenate(_identity_tables(ctx_len, 3) * 2, axis=1)
    tab_odd_ctx = jnp.concatenate(_identity_tables(ctx_len, 2), axis=1)

    x = x.reshape(n_lat, D_MODEL)
    ctx = ctx.reshape(n_ctx, D_MODEL)
    row = lambda v: v.reshape(1, -1)

    for layer in range(depth):
        need_ctx = layer < depth - 1
        i = layer // 2
        mod_lat = mods[layer, :batch]
        mod_ctx = mods[layer, batch:batch + 1]
        if layer % 2 == 0:
            win, wuq, wuk, wuv, wo_a, wo_b = _even_weights(even_w_in[i], even_w_out[i], mla_w_uq[i], mla_w_ukv[i])
            shared = (row(pre_mix_g[layer]), win, row(mla_q_norm_g[i]), wuq, row(mla_kv_norm_g[i]), wuk, wuv)
            qm, km, vm, qs, ks, vs = _pre_even(lat_rows, x, mod_lat, *shared, tab_even)
            qmc, kmc, vmc, qsc, ksc, vsc = _pre_even(ctx_rows, ctx, mod_ctx, *shared, tab_even_ctx)
            a_lat = _mla_attn(qm, kmc, vmc, km, vm, batch, seq, ctx_len, seq, 256)
            b_lat = _swa_attn_lat(swa_sink[i], qs, ks, vs, ksc, vsc, batch, seq, ctx_len, 512)
            if need_ctx:
                a_ctx = _mla_attn(qmc, kmc, vmc, None, None, batch, ctx_len, ctx_len, seq, ctx_len)
                b_ctx = _swa_attn_ctx(swa_sink[i], qsc, ksc, vsc, batch, ctx_len)
        else:
            win = odd_w_in[i].astype(BF16)
            n_a = NA_HEADS * NA_DIM
            wo_a, wo_b = odd_w_out[i][:n_a].astype(BF16), odd_w_out[i][n_a:].astype(BF16)
            shared = (row(pre_mix_g[layer]), win, row(gqa_q_g[i]), row(gqa_k_g[i]))
            qn, kn, vn, qq, kq, vq = _pre_odd(lat_rows, x, mod_lat, *shared, tab_odd)
            qnc, knc, vnc, qqc, kqc, vqc = _pre_odd(ctx_rows, ctx, mod_ctx, *shared, tab_odd_ctx)
            na_tab = _na_bias_table(na_rpb[i], seq // GRID_W)
            a_lat = _na_attn(na_tab, qn, kn, vn, knc, vnc, batch, seq, ctx_len)
            b_lat = _gqa_attn(qq, kqc, vqc, kq, vq, batch, seq, ctx_len, 256)
            assert not need_ctx, "context-query attention for odd layers is only needed when depth > 2"
        wg, wv, cw, wd = _ffn_weights(ffn_up[layer], ffn_conv_w[layer], ffn_conv_b[layer], ffn_down[layer])
        g1, g2, g3 = row(post_mix_g[layer]), row(pre_ffn_g[layer]), row(post_ffn_g[layer])
        x1, h = _post(lat_rows, a_lat, b_lat, x, mod_lat, wo_a, wo_b, g1, g2)
        x = _ffn(lat_rows, h, x1, mod_lat, g3, wg, wv, cw, wd)
        if need_ctx:
            c1, hc = _post(ctx_rows, a_ctx, b_ctx, ctx, mod_ctx, wo_a, wo_b, g1, g2)
            ctx = _ffn(ctx_rows, hc, c1, mod_ctx, g3, wg, wv, cw, wd)
    return x.reshape(batch, seq, D_MODEL)
```

```python
import functools
import math

import jax
import jax.numpy as jnp
import numpy as np
from jax import lax
from jax.experimental import pallas as pl
from jax.experimental.pallas import tpu as pltpu

F32 = jnp.float32
BF16 = jnp.bfloat16

D_MODEL = 1024
GRID_W = 64
ROPE_BASE = 10000.0
EPS = 1e-6
NEG_INF = -1e30
LOG2E = math.log2(math.e)
KEY_CHUNK = 512
LANES = 128

MLA_HEADS, MLA_Q_RANK, MLA_KV_RANK, MLA_NOPE, MLA_ROPE, MLA_V = 8, 256, 128, 64, 32, 64
SWA_HEADS, SWA_KV_HEADS, SWA_DIM, SWA_WINDOW = 8, 2, 64, 128
NA_HEADS, NA_DIM, NA_KH, NA_KW = 8, 64, 8, 16
GQA_HEADS, GQA_KV_HEADS, GQA_DIM = 4, 2, 128
D_FF = 2816
FF_CHUNK = 256
N_FF_CHUNKS = D_FF // FF_CHUNK
HALO = 16

NA_TILE_ROWS = 4
NA_WIN_ROWS = NA_TILE_ROWS + NA_KH - 1

VMEM_LIMIT = 56 * 1024 * 1024


def _params():
    return pltpu.CompilerParams(vmem_limit_bytes=VMEM_LIMIT)


def _dot(a, b):
    return jnp.dot(a, b, preferred_element_type=F32)


def _dot_nt(a, b):
    return lax.dot_general(a, b, (((1,), (1,)), ((), ())), preferred_element_type=F32)


def _rms(x, g):
    return x * lax.rsqrt(jnp.mean(x * x, axis=-1, keepdims=True) + EPS) * g


def _rope(x, c, sa, sb, sha, shb):
    return x * c + pltpu.roll(x, sha, axis=1) * sa + pltpu.roll(x, shb, axis=1) * sb


def _lane_lo(shape):
    return lax.broadcasted_iota(jnp.int32, shape, len(shape) - 1) < (LANES // 2)


def _const_spec(shape):
    nd = len(shape)
    return pl.BlockSpec(shape, lambda *_: (0,) * nd)


def _ada_kernel(cc_ref, w_ref, b_ref, o_ref):
    cc = cc_ref[...]
    s = cc * (1.0 / (1.0 + jnp.exp(-cc)))
    o_ref[0] = _dot(s.astype(BF16), w_ref[0].astype(BF16)) + b_ref[0]


def _ada(cc, ada_w, ada_b):
    depth = ada_w.shape[0]
    rows = cc.shape[0]
    return pl.pallas_call(
        _ada_kernel,
        grid=(depth, 6),
        in_specs=[
            pl.BlockSpec((rows, D_MODEL), lambda l, j: (0, 0)),
            pl.BlockSpec((1, D_MODEL, D_MODEL), lambda l, j: (l, 0, j)),
            pl.BlockSpec((1, 1, D_MODEL), lambda l, j: (l, 0, j)),
        ],
        out_specs=pl.BlockSpec((1, rows, D_MODEL), lambda l, j: (l, 0, j)),
        out_shape=jax.ShapeDtypeStruct((depth, rows, 6 * D_MODEL), F32),
        compiler_params=_params(),
        name="ada",
    )(cc, ada_w, ada_b.reshape(depth, 1, 6 * D_MODEL))


class _Rows:
    def __init__(self, n, seq, tm, per_batch_mod):
        assert seq % tm == 0 and n % seq == 0
        self.n, self.seq, self.tm = n, seq, tm
        self.tps = seq // tm
        self.grid = (n // tm,)
        self.per_batch_mod = per_batch_mod

    def row_spec(self, width):
        return pl.BlockSpec((self.tm, width), lambda i: (i, 0))

    def mod_spec(self):
        if self.per_batch_mod:
            tps = self.tps
            return pl.BlockSpec((1, 8, D_MODEL), lambda i: (i // tps, 0, 0))
        return pl.BlockSpec((1, 8, D_MODEL), lambda i: (0, 0, 0))

    def table_spec(self, width):
        tps = self.tps
        return pl.BlockSpec((self.tm, width), lambda i: (i % tps, 0))


def _modulated(x_ref, mod_ref, g_ref, shift_row, scale_row):
    x = x_ref[...]
    shift = mod_ref[0, shift_row:shift_row + 1, :]
    scale = mod_ref[0, scale_row:scale_row + 1, :]
    return _rms(x, g_ref[...]) * (1.0 + scale) + shift


EVEN_COLS = 1280


def _pre_even_kernel(x_ref, mod_ref, g_ref, win_ref, qg_ref, wuq_ref, kvg_ref, wuk_ref, wuv_ref,
                     tab_ref, qm_ref, km_ref, vm_ref, qs_ref, ks_ref, vs_ref):
    h = _modulated(x_ref, mod_ref, g_ref, 0, 1).astype(BF16)
    p = _dot(h, win_ref[...])
    mc, ma, mb = tab_ref[:, 0:128], tab_ref[:, 128:256], tab_ref[:, 256:384]
    sc, sa, sb = tab_ref[:, 384:512], tab_ref[:, 512:640], tab_ref[:, 640:768]

    cq = _rms(p[:, 0:256], qg_ref[...]).astype(BF16)
    q = _dot(cq, wuq_ref[...])
    mla_scale = LOG2E * (MLA_NOPE + MLA_ROPE) ** -0.5
    ckv = _rms(p[:, 256:384], kvg_ref[...]).astype(BF16)
    kn = _dot(ckv, wuk_ref[...])
    kpe = _rope(p[:, 384:512], mc, ma, mb, 112, 16)
    for j in range(MLA_HEADS):
        sl = slice(j * LANES, (j + 1) * LANES)
        qm_ref[:, sl] = (_rope(q[:, sl], mc, ma, mb, 112, 16) * mla_scale).astype(BF16)
        km_ref[:, sl] = (kn[:, sl] + kpe).astype(BF16)
    vm_ref[...] = _dot(ckv, wuv_ref[...]).astype(BF16)

    swa_scale = LOG2E * SWA_DIM ** -0.5
    for j in range(4):
        sl = slice(512 + j * LANES, 512 + (j + 1) * LANES)
        qs_ref[:, j * LANES:(j + 1) * LANES] = (_rope(p[:, sl], sc, sa, sb, 96, 32) * swa_scale).astype(BF16)
    ks_ref[...] = _rope(p[:, 1024:1152], sc, sa, sb, 96, 32).astype(BF16)
    vs_ref[...] = p[:, 1152:1280].astype(BF16)


def _pre_even(rows, x, mod, g, win, qg, wuq, kvg, wuk, wuv, tab):
    n = rows.n
    widths = (1024, 1024, 512, 512, 128, 128)
    return pl.pallas_call(
        _pre_even_kernel,
        grid=rows.grid,
        in_specs=[
            rows.row_spec(D_MODEL), rows.mod_spec(), _const_spec((1, D_MODEL)),
            _const_spec(win.shape), _const_spec(qg.shape), _const_spec(wuq.shape),
            _const_spec(kvg.shape), _const_spec(wuk.shape), _const_spec(wuv.shape),
            rows.table_spec(768),
        ],
        out_specs=[rows.row_spec(w) for w in widths],
        out_shape=[jax.ShapeDtypeStruct((n, w), BF16) for w in widths],
        compiler_params=_params(),
        name="pre_even",
    )(x, mod, g, win, qg, wuq, kvg, wuk, wuv, tab)


def _pre_odd_kernel(x_ref, mod_ref, g_ref, win_ref, qg_ref, kg_ref, tab_ref,
                    qn_ref, kn_ref, vn_ref, qq_ref, kq_ref, vq_ref):
    h = _modulated(x_ref, mod_ref, g_ref, 0, 1).astype(BF16)
    p = _dot(h, win_ref[...])
    gc, gs = tab_ref[:, 0:128], tab_ref[:, 128:256]
    na_scale = LOG2E * NA_DIM ** -0.5
    qn_ref[...] = (p[:, 0:512] * na_scale).astype(BF16)
    kn_ref[...] = p[:, 512:1024].astype(BF16)
    vn_ref[...] = p[:, 1024:1536].astype(BF16)
    gqa_scale = LOG2E * GQA_DIM ** -0.5

    def normed(blk, gain):
        y = _rms(blk, gain)
        return y * gc + pltpu.roll(y, 64, axis=1) * gs

    for j in range(GQA_HEADS):
        sl = slice(1536 + j * LANES, 1536 + (j + 1) * LANES)
        qq_ref[:, j * LANES:(j + 1) * LANES] = (normed(p[:, sl], qg_ref[...]) * gqa_scale).astype(BF16)
    for j in range(GQA_KV_HEADS):
        sl = slice(2048 + j * LANES, 2048 + (j + 1) * LANES)
        kq_ref[:, j * LANES:(j + 1) * LANES] = normed(p[:, sl], kg_ref[...]).astype(BF16)
    vq_ref[...] = p[:, 2304:2560].astype(BF16)


def _pre_odd(rows, x, mod, g, win, qg, kg, tab):
    n = rows.n
    widths = (512, 512, 512, 512, 256, 256)
    return pl.pallas_call(
        _pre_odd_kernel,
        grid=rows.grid,
        in_specs=[
            rows.row_spec(D_MODEL), rows.mod_spec(), _const_spec((1, D_MODEL)),
            _const_spec(win.shape), _const_spec(qg.shape), _const_spec(kg.shape),
            rows.table_spec(256),
        ],
        out_specs=[rows.row_spec(w) for w in widths],
        out_shape=[jax.ShapeDtypeStruct((n, w), BF16) for w in widths],
        compiler_params=_params(),
        name="pre_odd",
    )(x, mod, g, win, qg, kg, tab)


def _attend(q, chunks, sink=None):
    rows = q.shape[0]
    if sink is None:
        m, acc = None, None
    else:
        m = sink
        acc = jnp.concatenate([jnp.zeros((rows, LANES), F32), jnp.ones((rows, LANES), F32)], axis=1)
    for k, v, bias in chunks:
        s = _dot_nt(q, k)
        if bias is not None:
            s = s + bias
        cm = jnp.max(s, axis=-1, keepdims=True)
        m_new = cm if m is None else jnp.maximum(m, cm)
        p = jnp.exp2(s - m_new).astype(BF16)
        vx = jnp.concatenate([v, jnp.ones(v.shape, BF16)], axis=1)
        pv = _dot(p, vx)
        acc = pv if acc is None else jnp.exp2(m - m_new) * acc + pv
        m = m_new
    return acc[:, :LANES] / acc[:, LANES:]


def _chunks(k_ref, v_ref, lanes, size):
    n = k_ref.shape[0]
    assert n % size == 0
    return [(k_ref[c * size:(c + 1) * size, lanes], v_ref[c * size:(c + 1) * size, :], None)
            for c in range(n // size)]


def _mla_kernel(has_lat, q_ref, kc_ref, vc_ref, *rest):
    if has_lat:
        kl_ref, vl_ref, o_ref = rest
    else:
        (o_ref,) = rest
    outs = []
    for e in range(2):
        sl = slice(e * LANES, (e + 1) * LANES)
        chunks = _chunks(kc_ref, vc_ref, sl, kc_ref.shape[0])
        if has_lat:
            chunks += _chunks(kl_ref, vl_ref, sl, KEY_CHUNK)
        outs.append(_attend(q_ref[:, sl], chunks))
    o_ref[...] = jnp.where(_lane_lo(outs[0].shape), outs[0], outs[1]).astype(BF16)


def _mla_attn(q, kc, vc, kl, vl, batch, q_seq, ctx_len, lat_seq, tq):
    has_lat = kl is not None
    nq = q_seq // tq
    in_specs = [
        pl.BlockSpec((tq, 256), lambda b, j, t: (b * nq + t, j)),
        pl.BlockSpec((ctx_len, 256), lambda b, j, t: (b, j)),
        pl.BlockSpec((ctx_len, 128), lambda b, j, t: (b, j)),
    ]
    args = [q, kc, vc]
    if has_lat:
        in_specs += [
            pl.BlockSpec((lat_seq, 256), lambda b, j, t: (b, j)),
            pl.BlockSpec((lat_seq, 128), lambda b, j, t: (b, j)),
        ]
        args += [kl, vl]
    return pl.pallas_call(
        functools.partial(_mla_kernel, has_lat),
        grid=(batch, MLA_HEADS // 2, nq),
        in_specs=in_specs,
        out_specs=pl.BlockSpec((tq, 128), lambda b, j, t: (b * nq + t, j)),
        out_shape=jax.ShapeDtypeStruct((batch * q_seq, MLA_HEADS * MLA_V), BF16),
        compiler_params=_params(),
        name="mla_attn_lat" if has_lat else "mla_attn_ctx",
    )(*args)


def _gqa_kernel(q_ref, kc_ref, vc_ref, kl_ref, vl_ref, o_ref):
    tq = q_ref.shape[0]
    per_group = GQA_HEADS // GQA_KV_HEADS
    q = jnp.concatenate([q_ref[:, e * LANES:(e + 1) * LANES] for e in range(per_group)], axis=0)
    every = slice(None)
    chunks = _chunks(kc_ref, vc_ref, every, kc_ref.shape[0]) + _chunks(kl_ref, vl_ref, every, KEY_CHUNK)
    o = _attend(q, chunks).astype(BF16)
    for e in range(per_group):
        o_ref[:, e * LANES:(e + 1) * LANES] = o[e * tq:(e + 1) * tq]


def _gqa_attn(q, kc, vc, kl, vl, batch, seq, ctx_len, tq):
    nq = seq // tq
    return pl.pallas_call(
        _gqa_kernel,
        grid=(batch, GQA_KV_HEADS, nq),
        in_specs=[
            pl.BlockSpec((tq, 256), lambda b, g, t: (b * nq + t, g)),
            pl.BlockSpec((ctx_len, 128), lambda b, g, t: (b, g)),
            pl.BlockSpec((ctx_len, 128), lambda b, g, t: (b, g)),
            pl.BlockSpec((seq, 128), lambda b, g, t: (b, g)),
            pl.BlockSpec((seq, 128), lambda b, g, t: (b, g)),
        ],
        out_specs=pl.BlockSpec((tq, 256), lambda b, g, t: (b * nq + t, g)),
        out_shape=jax.ShapeDtypeStruct((batch * seq, GQA_HEADS * GQA_DIM), BF16),
        compiler_params=_params(),
        name="gqa_attn",
    )(q, kc, vc, kl, vl)


SWA_QB = 128
SWA_GROUP = SWA_HEADS // SWA_KV_HEADS


def _swa_group_lhs(q_ref, r0, rows, g):
    lo = _lane_lo((rows, LANES))
    keep = lo if g == 0 else jnp.logical_not(lo)
    blocks = [jnp.where(keep, q_ref[r0:r0 + rows, j * LANES:(j + 1) * LANES], jnp.zeros((), BF16))
              for j in range(SWA_GROUP)]
    return jnp.concatenate(blocks, axis=0)


def _swa_sink_col(sink_ref, g, rows):
    blk = lax.broadcasted_iota(jnp.int32, (SWA_GROUP * rows, 1), 0) // rows
    col = jnp.full((SWA_GROUP * rows, 1), sink_ref[g * SWA_GROUP], F32)
    for j in range(1, SWA_GROUP):
        col = jnp.where(blk == j, sink_ref[g * SWA_GROUP + j], col)
    return col * LOG2E


def _swa_store(o_ref, r0, rows, o_groups):
    lo = _lane_lo((rows, LANES))
    for j in range(SWA_GROUP):
        a = o_groups[0][j * rows:(j + 1) * rows]
        b = o_groups[1][j * rows:(j + 1) * rows]
        o_ref[r0:r0 + rows, j * LANES:(j + 1) * LANES] = jnp.where(lo, a, b).astype(BF16)


SWA_BAND = SWA_QB + 2 * SWA_WINDOW


def _swa_mask_table():
    ii = np.arange(SWA_GROUP * SWA_QB)[:, None] % SWA_QB
    jj = np.arange(SWA_BAND)[None, :]
    in_window = np.abs(jj - SWA_WINDOW - ii) <= SWA_WINDOW
    first = in_window & (jj >= SWA_WINDOW)
    last = in_window & (jj < SWA_WINDOW + SWA_QB)
    return jnp.asarray(np.where(np.stack([first, in_window, last]), 0.0, NEG_INF), F32)


def _swa_lat_kernel(tq, sink_ref, mask_ref, q_ref, kp_ref, kt_ref, kn_ref, vp_ref, vt_ref, vn_ref,
                    kc_ref, vc_ref, o_ref):
    t = pl.program_id(1)
    kband = jnp.concatenate([kp_ref[...], kt_ref[...], kn_ref[...]], axis=0)
    vband = jnp.concatenate([vp_ref[...], vt_ref[...], vn_ref[...]], axis=0)
    band = SWA_BAND
    n_blocks = tq // SWA_QB
    for i in range(n_blocks):
        variant = 1
        if i == 0:
            variant = jnp.where(t == 0, 0, variant)
        if i == n_blocks - 1:
            variant = jnp.where(t == pl.num_programs(1) - 1, 2, variant)
        bias = mask_ref[variant]
        kb = kband[i * SWA_QB:i * SWA_QB + band]
        vb = vband[i * SWA_QB:i * SWA_QB + band]
        o_groups = []
        for g in range(SWA_KV_HEADS):
            lhs = _swa_group_lhs(q_ref, i * SWA_QB, SWA_QB, g)
            o_groups.append(_attend(lhs, [(kc_ref[...], vc_ref[...], None), (kb, vb, bias)],
                                    _swa_sink_col(sink_ref, g, SWA_QB)))
        _swa_store(o_ref, i * SWA_QB, SWA_QB, o_groups)


def _swa_ctx_kernel(ctx_len, sink_ref, q_ref, kc_ref, vc_ref, o_ref):
    o_groups = []
    for g in range(SWA_KV_HEADS):
        lhs = _swa_group_lhs(q_ref, 0, ctx_len, g)
        o_groups.append(_attend(lhs, [(kc_ref[...], vc_ref[...], None)], _swa_sink_col(sink_ref, g, ctx_len)))
    _swa_store(o_ref, 0, ctx_len, o_groups)


def _swa_attn_lat(sink, q, k, v, kc, vc, batch, seq, ctx_len, tq):
    nq = seq // tq
    nb = seq // SWA_WINDOW
    per = tq // SWA_WINDOW
    prev_spec = pl.BlockSpec((SWA_WINDOW, 128), lambda b, t: (b * nb + jnp.maximum(t * per - 1, 0), 0))
    next_spec = pl.BlockSpec((SWA_WINDOW, 128), lambda b, t: (b * nb + jnp.minimum(t * per + per, nb - 1), 0))
    tile_spec = pl.BlockSpec((tq, 128), lambda b, t: (b * nq + t, 0))
    ctx_spec = pl.BlockSpec((ctx_len, 128), lambda b, t: (b, 0))
    mask = _swa_mask_table()
    return pl.pallas_call(
        functools.partial(_swa_lat_kernel, tq),
        grid=(batch, nq),
        in_specs=[
            pl.BlockSpec(memory_space=pltpu.SMEM),
            pl.BlockSpec(mask.shape, lambda b, t: (0, 0, 0)),
            pl.BlockSpec((tq, 512), lambda b, t: (b * nq + t, 0)),
            prev_spec, tile_spec, next_spec, prev_spec, tile_spec, next_spec,
            ctx_spec, ctx_spec,
        ],
        out_specs=pl.BlockSpec((tq, 512), lambda b, t: (b * nq + t, 0)),
        out_shape=jax.ShapeDtypeStruct((batch * seq, SWA_HEADS * SWA_DIM), BF16),
        compiler_params=_params(),
        name="swa_attn_lat",
    )(sink, mask, q, k, k, k, v, v, v, kc, vc)


def _swa_attn_ctx(sink, q, kc, vc, batch, ctx_len):
    return pl.pallas_call(
        functools.partial(_swa_ctx_kernel, ctx_len),
        grid=(batch,),
        in_specs=[
            pl.BlockSpec(memory_space=pltpu.SMEM),
            pl.BlockSpec((ctx_len, 512), lambda b: (b, 0)),
            pl.BlockSpec((ctx_len, 128), lambda b: (b, 0)),
            pl.BlockSpec((ctx_len, 128), lambda b: (b, 0)),
        ],
        out_specs=pl.BlockSpec((ctx_len, 512), lambda b: (b, 0)),
        out_shape=jax.ShapeDtypeStruct((batch * ctx_len, SWA_HEADS * SWA_DIM), BF16),
        compiler_params=_params(),
        name="swa_attn_ctx",
    )(sink, q, kc, vc)


NA_TILES_PER_STEP = 4


def _na_kernel(n_tiles, tab_ref, q_ref, k_ref, v_ref, kc_ref, vc_ref, o_ref):
    grid_rows = n_tiles * NA_TILE_ROWS
    tq = NA_TILE_ROWS * GRID_W
    lo = _lane_lo((tq, LANES))
    for u in range(NA_TILES_PER_STEP):
        t = pl.program_id(2) * NA_TILES_PER_STEP + u
        ws = jnp.clip(t * NA_TILE_ROWS - NA_KH // 2, 0, grid_rows - NA_WIN_ROWS)
        variant = jnp.where(t == 0, 0, jnp.where(t == n_tiles - 1, 2, 1))
        start = pl.multiple_of(ws * GRID_W, GRID_W)
        kwin = k_ref[pl.ds(start, NA_WIN_ROWS * GRID_W), :]
        vwin = v_ref[pl.ds(start, NA_WIN_ROWS * GRID_W), :]
        q = q_ref[u * tq:(u + 1) * tq, :]
        zero = jnp.zeros((), BF16)
        lhs = jnp.concatenate([jnp.where(lo, q, zero), jnp.where(lo, zero, q)], axis=0)
        bias = tab_ref[variant].reshape(2 * tq, NA_WIN_ROWS * GRID_W)
        o = _attend(lhs, [(kc_ref[...], vc_ref[...], None), (kwin, vwin, bias)])
        o_ref[u * tq:(u + 1) * tq, :] = jnp.where(lo, o[:tq], o[tq:]).astype(BF16)


def _na_attn(tab, q, k, v, kc, vc, batch, seq, ctx_len):
    tq = NA_TILE_ROWS * GRID_W
    n_tiles = seq // tq
    nk = NA_WIN_ROWS * GRID_W
    assert n_tiles % NA_TILES_PER_STEP == 0
    n_steps = n_tiles // NA_TILES_PER_STEP
    tq_step = tq * NA_TILES_PER_STEP
    return pl.pallas_call(
        functools.partial(_na_kernel, n_tiles),
        grid=(NA_HEADS // 2, batch, n_steps),
        in_specs=[
            pl.BlockSpec((3, 2, tq, nk), lambda j, b, t: (0, j, 0, 0)),
            pl.BlockSpec((tq_step, 128), lambda j, b, t: (b * n_steps + t, j)),
            pl.BlockSpec((seq, 128), lambda j, b, t: (b, j)),
            pl.BlockSpec((seq, 128), lambda j, b, t: (b, j)),
            pl.BlockSpec((ctx_len, 128), lambda j, b, t: (b, j)),
            pl.BlockSpec((ctx_len, 128), lambda j, b, t: (b, j)),
        ],
        out_specs=pl.BlockSpec((tq_step, 128), lambda j, b, t: (b * n_steps + t, j)),
        out_shape=jax.ShapeDtypeStruct((batch * seq, NA_HEADS * NA_DIM), BF16),
        compiler_params=_params(),
        name="na_attn",
    )(tab, q, k, v, kc, vc)


def _na_bias_table(rpb, grid_rows):
    cols = np.arange(GRID_W)
    col_start = np.clip(cols - NA_KW // 2, 0, GRID_W - NA_KW)
    col_ok = (cols[None, :] >= col_start[:, None]) & (cols[None, :] < col_start[:, None] + NA_KW)
    dc = np.clip(cols[None, :] - cols[:, None], -(NA_KW - 1), NA_KW - 1) + NA_KW - 1
    n_heads, n_dr = rpb.shape[0], rpb.shape[1]
    by_col = jnp.take(rpb.astype(F32) * LOG2E, jnp.asarray(dc.reshape(-1)), axis=2).reshape(n_heads, n_dr, GRID_W, GRID_W)
    by_col = jnp.where(jnp.asarray(col_ok)[None, None], by_col, NEG_INF)
    by_col = jnp.concatenate([by_col, jnp.full((n_heads, 1, GRID_W, GRID_W), NEG_INF, F32)], axis=1)
    n_tiles = grid_rows // NA_TILE_ROWS
    variants = []
    for t in (0, min(2, n_tiles - 1), n_tiles - 1):
        r0 = t * NA_TILE_ROWS
        ws = int(np.clip(r0 - NA_KH // 2, 0, grid_rows - NA_WIN_ROWS))
        r = r0 + np.arange(NA_TILE_ROWS)
        rs = np.clip(r - NA_KH // 2, 0, grid_rows - NA_KH)
        key_row = ws + np.arange(NA_WIN_ROWS)
        row_ok = (key_row[None, :] >= rs[:, None]) & (key_row[None, :] < rs[:, None] + NA_KH)
        dr = np.where(row_ok, key_row[None, :] - r[:, None] + NA_KH - 1, n_dr)
        rows = []
        for i in range(NA_TILE_ROWS):
            rows.append(jnp.concatenate([by_col[:, int(d)] for d in dr[i]], axis=-1))
        variants.append(jnp.concatenate(rows, axis=1))
    return jnp.stack(variants, axis=0)


def _post_kernel(a_ref, b_ref, x_ref, mod_ref, wa_ref, wb_ref, g1_ref, g2_ref, x1_ref, h_ref):
    y = _dot(a_ref[...], wa_ref[...]) + _dot(b_ref[...], wb_ref[...])
    x1 = x_ref[...] + mod_ref[0, 2:3, :] * _rms(y, g1_ref[...])
    x1_ref[...] = x1
    h_ref[...] = (_rms(x1, g2_ref[...]) * (1.0 + mod_ref[0, 4:5, :]) + mod_ref[0, 3:4, :]).astype(BF16)


def _post(rows, a, b, x, mod, wa, wb, g1, g2):
    n = rows.n
    return pl.pallas_call(
        _post_kernel,
        grid=rows.grid,
        in_specs=[
            rows.row_spec(a.shape[1]), rows.row_spec(b.shape[1]), rows.row_spec(D_MODEL), rows.mod_spec(),
            _const_spec(wa.shape), _const_spec(wb.shape), _const_spec((1, D_MODEL)), _const_spec((1, D_MODEL)),
        ],
        out_specs=[rows.row_spec(D_MODEL), rows.row_spec(D_MODEL)],
        out_shape=[jax.ShapeDtypeStruct((n, D_MODEL), F32), jax.ShapeDtypeStruct((n, D_MODEL), BF16)],
        compiler_params=_params(),
        name="post",
    )(a, b, x, mod, wa, wb, g1, g2)


def _gelu_tanh(x):
    return x * (0.5 * (1.0 + jnp.tanh(math.sqrt(2.0 / math.pi) * (x + 0.044715 * (x * x * x)))))


def _ffn_kernel(tm, tps, hp_ref, h_ref, hn_ref, x_ref, mod_ref, g_ref, wg_ref, wv_ref, cw_ref, wd_ref,
                o_ref, hh_ref, ug_ref, uv_ref, acc_ref):
    ti = pl.program_id(0) % tps
    hh_ref[0:HALO] = jnp.where(ti == 0, jnp.zeros((), BF16), hp_ref[...])
    hh_ref[HALO:HALO + tm] = h_ref[...]
    hh_ref[HALO + tm:] = jnp.where(ti == tps - 1, jnp.zeros((), BF16), hn_ref[...])
    acc_ref[...] = jnp.zeros_like(acc_ref)

    def up(c, slot):
        ug_ref[slot] = _dot(hh_ref[...], wg_ref[c])
        uv_ref[slot] = _dot(h_ref[...], wv_ref[c])

    def down(c, slot):
        cw = cw_ref[c]
        gate = cw[3:4]
        for tap in range(3):
            gate = gate + ug_ref[slot, HALO - 1 + tap:HALO - 1 + tap + tm] * cw[tap:tap + 1]
        act = (_gelu_tanh(gate) * uv_ref[slot]).astype(BF16)
        acc_ref[...] += _dot(act, wd_ref[c])

    up(0, 0)

    def step(i, carry):
        c = 2 * i
        up(c + 1, 1)
        down(c, 0)
        up(c + 2, 0)
        down(c + 1, 1)
        return carry

    assert N_FF_CHUNKS % 2 == 1
    lax.fori_loop(0, N_FF_CHUNKS // 2, step, 0)
    down(N_FF_CHUNKS - 1, 0)
    o_ref[...] = x_ref[...] + mod_ref[0, 5:6, :] * _rms(acc_ref[...], g_ref[...])


def _ffn(rows, h, x1, mod, g, wg, wv, cw, wd):
    n, tm, tps = rows.n, rows.tm, rows.tps
    per = tm // HALO
    last = n // HALO - 1
    return pl.pallas_call(
        functools.partial(_ffn_kernel, tm, tps),
        grid=rows.grid,
        in_specs=[
            pl.BlockSpec((HALO, D_MODEL), lambda i: (jnp.maximum(i * per - 1, 0), 0)),
            rows.row_spec(D_MODEL),
            pl.BlockSpec((HALO, D_MODEL), lambda i: (jnp.minimum((i + 1) * per, last), 0)),
            rows.row_spec(D_MODEL), rows.mod_spec(), _const_spec((1, D_MODEL)),
            _const_spec(wg.shape), _const_spec(wv.shape), _const_spec(cw.shape), _const_spec(wd.shape),
        ],
        out_specs=rows.row_spec(D_MODEL),
        out_shape=jax.ShapeDtypeStruct((n, D_MODEL), F32),
        scratch_shapes=[
            pltpu.VMEM((tm + 2 * HALO, D_MODEL), BF16),
            pltpu.VMEM((2, tm + 2 * HALO, FF_CHUNK), F32),
            pltpu.VMEM((2, tm, FF_CHUNK), F32),
            pltpu.VMEM((tm, D_MODEL), F32),
        ],
        compiler_params=_params(),
        name="ffn",
    )(h, h, h, x1, mod, g, wg, wv, cw, wd)


def _rope_angles(n, d):
    t = jnp.arange(n)
    rows = (t // GRID_W).astype(F32)
    cols = (t % GRID_W).astype(F32)
    freqs = ROPE_BASE ** (-jnp.arange(d // 4, dtype=F32) * 4.0 / d)
    ang = jnp.concatenate([rows[:, None] * freqs, cols[:, None] * freqs], axis=-1)
    return jnp.cos(ang), jnp.sin(ang)


def _rope_lane_tables(cos, sin, offset, group):
    n, half = cos.shape
    c = jnp.ones((n, group), F32)
    sa = jnp.zeros((n, group), F32)
    sb = jnp.zeros((n, group), F32)
    c = c.at[:, offset:offset + half].set(cos).at[:, offset + half:offset + 2 * half].set(cos)
    sa = sa.at[:, offset:offset + half].set(-sin)
    sb = sb.at[:, offset + half:offset + 2 * half].set(sin)
    reps = LANES // group
    return tuple(jnp.tile(a, (1, reps)) for a in (c, sa, sb))


def _identity_tables(n, width_blocks):
    ones = jnp.ones((n, LANES), F32)
    zeros = jnp.zeros((n, LANES), F32)
    return [ones, zeros, zeros][:width_blocks]


def _even_weights(w_in, w_out, w_uq, w_ukv):
    z = lambda r, c: jnp.zeros((r, c), F32)
    mla_in = MLA_Q_RANK + MLA_KV_RANK + MLA_ROPE
    kpe = w_in[:, MLA_Q_RANK + MLA_KV_RANK:mla_in]
    swa = w_in[:, mla_in:]
    swa_q = swa[:, :SWA_HEADS * SWA_DIM].reshape(D_MODEL, SWA_HEADS, SWA_DIM)
    pair_order = np.array([h for j in range(SWA_GROUP) for h in (j, SWA_GROUP + j)])
    swa_q = swa_q[:, pair_order, :].reshape(D_MODEL, SWA_HEADS * SWA_DIM)
    win = jnp.concatenate([
        w_in[:, :MLA_Q_RANK + MLA_KV_RANK], z(D_MODEL, 64), kpe, z(D_MODEL, 32), swa_q,
        swa[:, SWA_HEADS * SWA_DIM:]], axis=1)
    assert win.shape[1] == EVEN_COLS
    wuq = w_uq.reshape(MLA_Q_RANK, MLA_HEADS, MLA_NOPE + MLA_ROPE)
    wuq = jnp.pad(wuq, ((0, 0), (0, 0), (0, LANES - MLA_NOPE - MLA_ROPE))).reshape(MLA_Q_RANK, MLA_HEADS * LANES)
    wukv = w_ukv.reshape(MLA_KV_RANK, MLA_HEADS, MLA_NOPE + MLA_V)
    wuk = jnp.pad(wukv[:, :, :MLA_NOPE], ((0, 0), (0, 0), (0, LANES - MLA_NOPE))).reshape(MLA_KV_RANK, MLA_HEADS * LANES)
    wuv = wukv[:, :, MLA_NOPE:].reshape(MLA_KV_RANK, MLA_HEADS * MLA_V)
    n_a = MLA_HEADS * MLA_V
    wo_a = w_out[:n_a]
    wo_b = w_out[n_a:].reshape(SWA_HEADS, SWA_DIM, D_MODEL)[pair_order].reshape(SWA_HEADS * SWA_DIM, D_MODEL)
    return tuple(a.astype(BF16) for a in (win, wuq, wuk, wuv, wo_a, wo_b))


def _ffn_weights(w_up, conv_w, conv_b, w_down):
    def chunks(w):
        return w.reshape(D_MODEL, N_FF_CHUNKS, FF_CHUNK).transpose(1, 0, 2).astype(BF16)

    cw = jnp.concatenate([conv_w, conv_b[None], jnp.zeros((4, D_FF), F32)], axis=0)
    cw = cw.reshape(8, N_FF_CHUNKS, FF_CHUNK).transpose(1, 0, 2)
    wd = w_down.reshape(N_FF_CHUNKS, FF_CHUNK, D_MODEL).astype(BF16)
    return chunks(w_up[:, :D_FF]), chunks(w_up[:, D_FF:]), cw, wd


def kernel(x, c, ctx, c_ctx, ada_w, ada_b, pre_mix_g, post_mix_g, pre_ffn_g, post_ffn_g, ffn_up, ffn_conv_w,
           ffn_conv_b, ffn_down, even_w_in, even_w_out, mla_q_norm_g, mla_w_uq, mla_kv_norm_g, mla_w_ukv,
           swa_sink, odd_w_in, odd_w_out, na_rpb, gqa_q_g, gqa_k_g):
    batch, seq, _ = x.shape
    ctx_len = ctx.shape[1]
    depth = ada_w.shape[0]
    n_lat, n_ctx = batch * seq, batch * ctx_len

    pad_rows = -(batch + 1) % 8
    cc = jnp.concatenate([c, c_ctx[None], jnp.zeros((pad_rows, D_MODEL), F32)], axis=0)
    mods = _ada(cc, ada_w, ada_b).reshape(depth, cc.shape[0], 6, D_MODEL)
    mods = jnp.pad(mods, ((0, 0), (0, 0), (0, 2), (0, 0)))

    lat_rows = _Rows(n_lat, seq, 512, True)
    ctx_rows = _Rows(n_ctx, ctx_len, ctx_len, False)

    cos_m, sin_m = _rope_angles(seq, MLA_ROPE)
    cos_s, sin_s = _rope_angles(seq, SWA_DIM)
    cos_g, sin_g = _rope_angles(seq, GQA_DIM)
    gc, gsa, gsb = _rope_lane_tables(cos_g, sin_g, 0, LANES)
    tab_even = jnp.concatenate(_rope_lane_tables(cos_m, sin_m, MLA_NOPE, LANES)
                               + _rope_lane_tables(cos_s, sin_s, 0, SWA_DIM), axis=1)
    tab_odd = jnp.concatenate([gc, gsa + gsb], axis=1)
    tab_even_ctx = jnp.concatenate(_identity_tables(ctx_len, 3) * 2, axis=1)
    tab_odd_ctx = jnp.concatenate(_identity_tables(ctx_len, 2), axis=1)

    x = x.reshape(n_lat, D_MODEL)
    ctx = ctx.reshape(n_ctx, D_MODEL)
    row = lambda v: v.reshape(1, -1)

    for layer in range(depth):
        need_ctx = layer < depth - 1
        i = layer // 2
        mod_lat = mods[layer, :batch]
        mod_ctx = mods[layer, batch:batch + 1]
        if layer % 2 == 0:
            win, wuq, wuk, wuv, wo_a, wo_b = _even_weights(even_w_in[i], even_w_out[i], mla_w_uq[i], mla_w_ukv[i])
            shared = (row(pre_mix_g[layer]), win, row(mla_q_norm_g[i]), wuq, row(mla_kv_norm_g[i]), wuk, wuv)
            qm, km, vm, qs, ks, vs = _pre_even(lat_rows, x, mod_lat, *shared, tab_even)
            qmc, kmc, vmc, qsc, ksc, vsc = _pre_even(ctx_rows, ctx, mod_ctx, *shared, tab_even_ctx)
            a_lat = _mla_attn(qm, kmc, vmc, km, vm, batch, seq, ctx_len, seq, 512)
            b_lat = _swa_attn_lat(swa_sink[i], qs, ks, vs, ksc, vsc, batch, seq, ctx_len, 512)
            if need_ctx:
                a_ctx = _mla_attn(qmc, kmc, vmc, None, None, batch, ctx_len, ctx_len, seq, ctx_len)
                b_ctx = _swa_attn_ctx(swa_sink[i], qsc, ksc, vsc, batch, ctx_len)
        else:
            win = odd_w_in[i].astype(BF16)
            n_a = NA_HEADS * NA_DIM
            wo_a, wo_b = odd_w_out[i][:n_a].astype(BF16), odd_w_out[i][n_a:].astype(BF16)
            shared = (row(pre_mix_g[layer]), win, row(gqa_q_g[i]), row(gqa_k_g[i]))
            qn, kn, vn, qq, kq, vq = _pre_odd(lat_rows, x, mod_lat, *shared, tab_odd)
            qnc, knc, vnc, qqc, kqc, vqc = _pre_odd(ctx_rows, ctx, mod_ctx, *shared, tab_odd_ctx)
            na_tab = _na_bias_table(na_rpb[i], seq // GRID_W)
            a_lat = _na_attn(na_tab, qn, kn, vn, knc, vnc, batch, seq, ctx_len)
            b_lat = _gqa_attn(qq, kqc, vqc, kq, vq, batch, seq, ctx_len, 512)
            assert not need_ctx, "context-query attention for odd layers is only needed when depth > 2"
        wg, wv, cw, wd = _ffn_weights(ffn_up[layer], ffn_conv_w[layer], ffn_conv_b[layer], ffn_down[layer])
        g1, g2, g3 = row(post_mix_g[layer]), row(pre_ffn_g[layer]), row(post_ffn_g[layer])
        x1, h = _post(lat_rows, a_lat, b_lat, x, mod_lat, wo_a, wo_b, g1, g2)
        x = _ffn(lat_rows, h, x1, mod_lat, g3, wg, wv, cw, wd)
        if need_ctx:
            c1, hc = _post(ctx_rows, a_ctx, b_ctx, ctx, mod_ctx, wo_a, wo_b, g1, g2)
            ctx = _ffn(ctx_rows, hc, c1, mod_ctx, g3, wg, wv, cw, wd)
    return x.reshape(batch, seq, D_MODEL)
```

```python
import functools
import math

import jax
import jax.numpy as jnp
import numpy as np
from jax import lax
from jax.experimental import pallas as pl
from jax.experimental.pallas import tpu as pltpu

F32 = jnp.float32
BF16 = jnp.bfloat16

D_MODEL = 1024
GRID_W = 64
ROPE_BASE = 10000.0
EPS = 1e-6
NEG_INF = -1e30
LOG2E = math.log2(math.e)
KEY_CHUNK = 512
LANES = 128

MLA_HEADS, MLA_Q_RANK, MLA_KV_RANK, MLA_NOPE, MLA_ROPE, MLA_V = 8, 256, 128, 64, 32, 64
SWA_HEADS, SWA_KV_HEADS, SWA_DIM, SWA_WINDOW = 8, 2, 64, 128
NA_HEADS, NA_DIM, NA_KH, NA_KW = 8, 64, 8, 16
GQA_HEADS, GQA_KV_HEADS, GQA_DIM = 4, 2, 128
D_FF = 2816
FF_CHUNK = 256
N_FF_CHUNKS = D_FF // FF_CHUNK
HALO = 16

NA_TILE_ROWS = 4
NA_WIN_ROWS = NA_TILE_ROWS + NA_KH - 1

VMEM_LIMIT = 56 * 1024 * 1024


def _params():
    return pltpu.CompilerParams(vmem_limit_bytes=VMEM_LIMIT)


def _dot(a, b):
    return jnp.dot(a, b, preferred_element_type=F32)


def _dot_nt(a, b):
    return lax.dot_general(a, b, (((1,), (1,)), ((), ())), preferred_element_type=F32)


def _rms(x, g):
    return x * lax.rsqrt(jnp.mean(x * x, axis=-1, keepdims=True) + EPS) * g


def _rope(x, c, sa, sb, sha, shb):
    return x * c + pltpu.roll(x, sha, axis=1) * sa + pltpu.roll(x, shb, axis=1) * sb


def _lane_lo(shape):
    return lax.broadcasted_iota(jnp.int32, shape, len(shape) - 1) < (LANES // 2)


def _const_spec(shape):
    nd = len(shape)
    return pl.BlockSpec(shape, lambda *_: (0,) * nd)


def _ada_kernel(cc_ref, w_ref, b_ref, o_ref):
    cc = cc_ref[...]
    s = cc * (1.0 / (1.0 + jnp.exp(-cc)))
    o_ref[0] = _dot(s.astype(BF16), w_ref[0].astype(BF16)) + b_ref[0]


def _ada(cc, ada_w, ada_b):
    depth = ada_w.shape[0]
    rows = cc.shape[0]
    return pl.pallas_call(
        _ada_kernel,
        grid=(depth, 6),
        in_specs=[
            pl.BlockSpec((rows, D_MODEL), lambda l, j: (0, 0)),
            pl.BlockSpec((1, D_MODEL, D_MODEL), lambda l, j: (l, 0, j)),
            pl.BlockSpec((1, 1, D_MODEL), lambda l, j: (l, 0, j)),
        ],
        out_specs=pl.BlockSpec((1, rows, D_MODEL), lambda l, j: (l, 0, j)),
        out_shape=jax.ShapeDtypeStruct((depth, rows, 6 * D_MODEL), F32),
        compiler_params=_params(),
        name="ada",
    )(cc, ada_w, ada_b.reshape(depth, 1, 6 * D_MODEL))


class _Rows:
    def __init__(self, n, seq, tm, per_batch_mod):
        assert seq % tm == 0 and n % seq == 0
        self.n, self.seq, self.tm = n, seq, tm
        self.tps = seq // tm
        self.grid = (n // tm,)
        self.per_batch_mod = per_batch_mod

    def row_spec(self, width):
        return pl.BlockSpec((self.tm, width), lambda i: (i, 0))

    def mod_spec(self):
        if self.per_batch_mod:
            tps = self.tps
            return pl.BlockSpec((1, 8, D_MODEL), lambda i: (i // tps, 0, 0))
        return pl.BlockSpec((1, 8, D_MODEL), lambda i: (0, 0, 0))

    def table_spec(self, width):
        tps = self.tps
        return pl.BlockSpec((self.tm, width), lambda i: (i % tps, 0))


PRE_HALF = 512


def _row_halves(tm):
    half = min(tm, PRE_HALF)
    assert tm % half == 0
    return [slice(r, r + half) for r in range(0, tm, half)]


def _modulated(x_ref, mod_ref, g_ref, rows, shift_row, scale_row):
    x = x_ref[rows, :]
    shift = mod_ref[0, shift_row:shift_row + 1, :]
    scale = mod_ref[0, scale_row:scale_row + 1, :]
    return _rms(x, g_ref[...]) * (1.0 + scale) + shift


EVEN_COLS = 1280


def _pre_even_kernel(x_ref, mod_ref, g_ref, win_ref, qg_ref, wuq_ref, kvg_ref, wuk_ref, wuv_ref,
                     tab_ref, qm_ref, km_ref, vm_ref, qs_ref, ks_ref, vs_ref):
    mla_scale = LOG2E * (MLA_NOPE + MLA_ROPE) ** -0.5
    swa_scale = LOG2E * SWA_DIM ** -0.5
    for rows in _row_halves(x_ref.shape[0]):
        h = _modulated(x_ref, mod_ref, g_ref, rows, 0, 1).astype(BF16)
        mc, ma, mb = tab_ref[rows, 0:128], tab_ref[rows, 128:256], tab_ref[rows, 256:384]
        sc, sa, sb = tab_ref[rows, 384:512], tab_ref[rows, 512:640], tab_ref[rows, 640:768]

        pa = _dot(h, win_ref[:, 0:512])
        cq = _rms(pa[:, 0:256], qg_ref[...]).astype(BF16)
        ckv = _rms(pa[:, 256:384], kvg_ref[...]).astype(BF16)
        kpe = _rope(pa[:, 384:512], mc, ma, mb, 112, 16)
        q = _dot(cq, wuq_ref[...])
        kn = _dot(ckv, wuk_ref[...])
        for j in range(MLA_HEADS):
            sl = slice(j * LANES, (j + 1) * LANES)
            qm_ref[rows, sl] = (_rope(q[:, sl], mc, ma, mb, 112, 16) * mla_scale).astype(BF16)
            km_ref[rows, sl] = (kn[:, sl] + kpe).astype(BF16)
        vm_ref[rows, :] = _dot(ckv, wuv_ref[...]).astype(BF16)

        pq = _dot(h, win_ref[:, 512:1024])
        for j in range(4):
            sl = slice(j * LANES, (j + 1) * LANES)
            qs_ref[rows, sl] = (_rope(pq[:, sl], sc, sa, sb, 96, 32) * swa_scale).astype(BF16)
        pkv = _dot(h, win_ref[:, 1024:1280])
        ks_ref[rows, :] = _rope(pkv[:, 0:128], sc, sa, sb, 96, 32).astype(BF16)
        vs_ref[rows, :] = pkv[:, 128:256].astype(BF16)


def _pre_even(rows, x, mod, g, win, qg, wuq, kvg, wuk, wuv, tab):
    n = rows.n
    widths = (1024, 1024, 512, 512, 128, 128)
    return pl.pallas_call(
        _pre_even_kernel,
        grid=rows.grid,
        in_specs=[
            rows.row_spec(D_MODEL), rows.mod_spec(), _const_spec((1, D_MODEL)),
            _const_spec(win.shape), _const_spec(qg.shape), _const_spec(wuq.shape),
            _const_spec(kvg.shape), _const_spec(wuk.shape), _const_spec(wuv.shape),
            rows.table_spec(768),
        ],
        out_specs=[rows.row_spec(w) for w in widths],
        out_shape=[jax.ShapeDtypeStruct((n, w), BF16) for w in widths],
        compiler_params=_params(),
        name="pre_even",
    )(x, mod, g, win, qg, wuq, kvg, wuk, wuv, tab)


def _pre_odd_kernel(x_ref, mod_ref, g_ref, win_ref, qg_ref, kg_ref, tab_ref,
                    qn_ref, kn_ref, vn_ref, qq_ref, kq_ref, vq_ref):
    na_scale = LOG2E * NA_DIM ** -0.5
    gqa_scale = LOG2E * GQA_DIM ** -0.5
    for rows in _row_halves(x_ref.shape[0]):
        h = _modulated(x_ref, mod_ref, g_ref, rows, 0, 1).astype(BF16)
        gc, gs = tab_ref[rows, 0:128], tab_ref[rows, 128:256]

        def normed(blk, gain, gc=gc, gs=gs):
            y = _rms(blk, gain)
            return y * gc + pltpu.roll(y, 64, axis=1) * gs

        qn_ref[rows, :] = (_dot(h, win_ref[:, 0:512]) * na_scale).astype(BF16)
        kn_ref[rows, :] = _dot(h, win_ref[:, 512:1024]).astype(BF16)
        vn_ref[rows, :] = _dot(h, win_ref[:, 1024:1536]).astype(BF16)
        pq = _dot(h, win_ref[:, 1536:2048])
        for j in range(GQA_HEADS):
            sl = slice(j * LANES, (j + 1) * LANES)
            qq_ref[rows, sl] = (normed(pq[:, sl], qg_ref[...]) * gqa_scale).astype(BF16)
        pkv = _dot(h, win_ref[:, 2048:2560])
        for j in range(GQA_KV_HEADS):
            sl = slice(j * LANES, (j + 1) * LANES)
            kq_ref[rows, sl] = normed(pkv[:, sl], kg_ref[...]).astype(BF16)
        vq_ref[rows, :] = pkv[:, 256:512].astype(BF16)


def _pre_odd(rows, x, mod, g, win, qg, kg, tab):
    n = rows.n
    widths = (512, 512, 512, 512, 256, 256)
    return pl.pallas_call(
        _pre_odd_kernel,
        grid=rows.grid,
        in_specs=[
            rows.row_spec(D_MODEL), rows.mod_spec(), _const_spec((1, D_MODEL)),
            _const_spec(win.shape), _const_spec(qg.shape), _const_spec(kg.shape),
            rows.table_spec(256),
        ],
        out_specs=[rows.row_spec(w) for w in widths],
        out_shape=[jax.ShapeDtypeStruct((n, w), BF16) for w in widths],
        compiler_params=_params(),
        name="pre_odd",
    )(x, mod, g, win, qg, kg, tab)


def _attend(q, chunks, sink=None):
    rows = q.shape[0]
    if sink is None:
        m, acc = None, None
    else:
        m = sink
        acc = jnp.concatenate([jnp.zeros((rows, LANES), F32), jnp.ones((rows, LANES), F32)], axis=1)
    for k, v, bias in chunks:
        s = _dot_nt(q, k)
        if bias is not None:
            s = s + bias
        cm = jnp.max(s, axis=-1, keepdims=True)
        m_new = cm if m is None else jnp.maximum(m, cm)
        p = jnp.exp2(s - m_new).astype(BF16)
        vx = jnp.concatenate([v, jnp.ones(v.shape, BF16)], axis=1)
        pv = _dot(p, vx)
        acc = pv if acc is None else jnp.exp2(m - m_new) * acc + pv
        m = m_new
    return acc[:, :LANES] / acc[:, LANES:]


def _chunks(k_ref, v_ref, lanes, size):
    n = k_ref.shape[0]
    assert n % size == 0
    return [(k_ref[c * size:(c + 1) * size, lanes], v_ref[c * size:(c + 1) * size, :], None)
            for c in range(n // size)]


def _mla_kernel(has_lat, q_ref, kc_ref, vc_ref, *rest):
    if has_lat:
        kl_ref, vl_ref, o_ref = rest
    else:
        (o_ref,) = rest
    outs = []
    for e in range(2):
        sl = slice(e * LANES, (e + 1) * LANES)
        chunks = _chunks(kc_ref, vc_ref, sl, kc_ref.shape[0])
        if has_lat:
            chunks += _chunks(kl_ref, vl_ref, sl, KEY_CHUNK)
        outs.append(_attend(q_ref[:, sl], chunks))
    o_ref[...] = jnp.where(_lane_lo(outs[0].shape), outs[0], outs[1]).astype(BF16)


def _mla_attn(q, kc, vc, kl, vl, batch, q_seq, ctx_len, lat_seq, tq):
    has_lat = kl is not None
    nq = q_seq // tq
    in_specs = [
        pl.BlockSpec((tq, 256), lambda b, j, t: (b * nq + t, j)),
        pl.BlockSpec((ctx_len, 256), lambda b, j, t: (b, j)),
        pl.BlockSpec((ctx_len, 128), lambda b, j, t: (b, j)),
    ]
    args = [q, kc, vc]
    if has_lat:
        in_specs += [
            pl.BlockSpec((lat_seq, 256), lambda b, j, t: (b, j)),
            pl.BlockSpec((lat_seq, 128), lambda b, j, t: (b, j)),
        ]
        args += [kl, vl]
    return pl.pallas_call(
        functools.partial(_mla_kernel, has_lat),
        grid=(batch, MLA_HEADS // 2, nq),
        in_specs=in_specs,
        out_specs=pl.BlockSpec((tq, 128), lambda b, j, t: (b * nq + t, j)),
        out_shape=jax.ShapeDtypeStruct((batch * q_seq, MLA_HEADS * MLA_V), BF16),
        compiler_params=_params(),
        name="mla_attn_lat" if has_lat else "mla_attn_ctx",
    )(*args)


def _gqa_kernel(q_ref, kc_ref, vc_ref, kl_ref, vl_ref, o_ref):
    tq = q_ref.shape[0]
    per_group = GQA_HEADS // GQA_KV_HEADS
    q = jnp.concatenate([q_ref[:, e * LANES:(e + 1) * LANES] for e in range(per_group)], axis=0)
    every = slice(None)
    chunks = _chunks(kc_ref, vc_ref, every, kc_ref.shape[0]) + _chunks(kl_ref, vl_ref, every, KEY_CHUNK)
    o = _attend(q, chunks).astype(BF16)
    for e in range(per_group):
        o_ref[:, e * LANES:(e + 1) * LANES] = o[e * tq:(e + 1) * tq]


def _gqa_attn(q, kc, vc, kl, vl, batch, seq, ctx_len, tq):
    nq = seq // tq
    return pl.pallas_call(
        _gqa_kernel,
        grid=(batch, GQA_KV_HEADS, nq),
        in_specs=[
            pl.BlockSpec((tq, 256), lambda b, g, t: (b * nq + t, g)),
            pl.BlockSpec((ctx_len, 128), lambda b, g, t: (b, g)),
            pl.BlockSpec((ctx_len, 128), lambda b, g, t: (b, g)),
            pl.BlockSpec((seq, 128), lambda b, g, t: (b, g)),
            pl.BlockSpec((seq, 128), lambda b, g, t: (b, g)),
        ],
        out_specs=pl.BlockSpec((tq, 256), lambda b, g, t: (b * nq + t, g)),
        out_shape=jax.ShapeDtypeStruct((batch * seq, GQA_HEADS * GQA_DIM), BF16),
        compiler_params=_params(),
        name="gqa_attn",
    )(q, kc, vc, kl, vl)


SWA_QB = 128
SWA_GROUP = SWA_HEADS // SWA_KV_HEADS


def _swa_group_lhs(q_ref, r0, rows, g):
    lo = _lane_lo((rows, LANES))
    keep = lo if g == 0 else jnp.logical_not(lo)
    blocks = [jnp.where(keep, q_ref[r0:r0 + rows, j * LANES:(j + 1) * LANES], jnp.zeros((), BF16))
              for j in range(SWA_GROUP)]
    return jnp.concatenate(blocks, axis=0)


def _swa_sink_col(sink_ref, g, rows):
    blk = lax.broadcasted_iota(jnp.int32, (SWA_GROUP * rows, 1), 0) // rows
    col = jnp.full((SWA_GROUP * rows, 1), sink_ref[g * SWA_GROUP], F32)
    for j in range(1, SWA_GROUP):
        col = jnp.where(blk == j, sink_ref[g * SWA_GROUP + j], col)
    return col * LOG2E


def _swa_store(o_ref, r0, rows, o_groups):
    lo = _lane_lo((rows, LANES))
    for j in range(SWA_GROUP):
        a = o_groups[0][j * rows:(j + 1) * rows]
        b = o_groups[1][j * rows:(j + 1) * rows]
        o_ref[r0:r0 + rows, j * LANES:(j + 1) * LANES] = jnp.where(lo, a, b).astype(BF16)


SWA_BAND = SWA_QB + 2 * SWA_WINDOW


def _swa_mask_table():
    ii = np.arange(SWA_GROUP * SWA_QB)[:, None] % SWA_QB
    jj = np.arange(SWA_BAND)[None, :]
    in_window = np.abs(jj - SWA_WINDOW - ii) <= SWA_WINDOW
    first = in_window & (jj >= SWA_WINDOW)
    last = in_window & (jj < SWA_WINDOW + SWA_QB)
    return jnp.asarray(np.where(np.stack([first, in_window, last]), 0.0, NEG_INF), F32)


def _swa_lat_kernel(tq, sink_ref, mask_ref, q_ref, kp_ref, kt_ref, kn_ref, vp_ref, vt_ref, vn_ref,
                    kc_ref, vc_ref, o_ref):
    t = pl.program_id(1)
    kband = jnp.concatenate([kp_ref[...], kt_ref[...], kn_ref[...]], axis=0)
    vband = jnp.concatenate([vp_ref[...], vt_ref[...], vn_ref[...]], axis=0)
    band = SWA_BAND
    n_blocks = tq // SWA_QB
    for i in range(n_blocks):
        variant = 1
        if i == 0:
            variant = jnp.where(t == 0, 0, variant)
        if i == n_blocks - 1:
            variant = jnp.where(t == pl.num_programs(1) - 1, 2, variant)
        bias = mask_ref[variant]
        kb = kband[i * SWA_QB:i * SWA_QB + band]
        vb = vband[i * SWA_QB:i * SWA_QB + band]
        o_groups = []
        for g in range(SWA_KV_HEADS):
            lhs = _swa_group_lhs(q_ref, i * SWA_QB, SWA_QB, g)
            o_groups.append(_attend(lhs, [(kc_ref[...], vc_ref[...], None), (kb, vb, bias)],
                                    _swa_sink_col(sink_ref, g, SWA_QB)))
        _swa_store(o_ref, i * SWA_QB, SWA_QB, o_groups)


def _swa_ctx_kernel(ctx_len, sink_ref, q_ref, kc_ref, vc_ref, o_ref):
    o_groups = []
    for g in range(SWA_KV_HEADS):
        lhs = _swa_group_lhs(q_ref, 0, ctx_len, g)
        o_groups.append(_attend(lhs, [(kc_ref[...], vc_ref[...], None)], _swa_sink_col(sink_ref, g, ctx_len)))
    _swa_store(o_ref, 0, ctx_len, o_groups)


def _swa_attn_lat(sink, q, k, v, kc, vc, batch, seq, ctx_len, tq):
    nq = seq // tq
    nb = seq // SWA_WINDOW
    per = tq // SWA_WINDOW
    prev_spec = pl.BlockSpec((SWA_WINDOW, 128), lambda b, t: (b * nb + jnp.maximum(t * per - 1, 0), 0))
    next_spec = pl.BlockSpec((SWA_WINDOW, 128), lambda b, t: (b * nb + jnp.minimum(t * per + per, nb - 1), 0))
    tile_spec = pl.BlockSpec((tq, 128), lambda b, t: (b * nq + t, 0))
    ctx_spec = pl.BlockSpec((ctx_len, 128), lambda b, t: (b, 0))
    mask = _swa_mask_table()
    return pl.pallas_call(
        functools.partial(_swa_lat_kernel, tq),
        grid=(batch, nq),
        in_specs=[
            pl.BlockSpec(memory_space=pltpu.SMEM),
            pl.BlockSpec(mask.shape, lambda b, t: (0, 0, 0)),
            pl.BlockSpec((tq, 512), lambda b, t: (b * nq + t, 0)),
            prev_spec, tile_spec, next_spec, prev_spec, tile_spec, next_spec,
            ctx_spec, ctx_spec,
        ],
        out_specs=pl.BlockSpec((tq, 512), lambda b, t: (b * nq + t, 0)),
        out_shape=jax.ShapeDtypeStruct((batch * seq, SWA_HEADS * SWA_DIM), BF16),
        compiler_params=_params(),
        name="swa_attn_lat",
    )(sink, mask, q, k, k, k, v, v, v, kc, vc)


def _swa_attn_ctx(sink, q, kc, vc, batch, ctx_len):
    return pl.pallas_call(
        functools.partial(_swa_ctx_kernel, ctx_len),
        grid=(batch,),
        in_specs=[
            pl.BlockSpec(memory_space=pltpu.SMEM),
            pl.BlockSpec((ctx_len, 512), lambda b: (b, 0)),
            pl.BlockSpec((ctx_len, 128), lambda b: (b, 0)),
            pl.BlockSpec((ctx_len, 128), lambda b: (b, 0)),
        ],
        out_specs=pl.BlockSpec((ctx_len, 512), lambda b: (b, 0)),
        out_shape=jax.ShapeDtypeStruct((batch * ctx_len, SWA_HEADS * SWA_DIM), BF16),
        compiler_params=_params(),
        name="swa_attn_ctx",
    )(sink, q, kc, vc)


NA_TILES_PER_STEP = 8


def _na_kernel(n_tiles, tab_ref, q_ref, k_ref, v_ref, kc_ref, vc_ref, o_ref):
    grid_rows = n_tiles * NA_TILE_ROWS
    tq = NA_TILE_ROWS * GRID_W
    lo = _lane_lo((tq, LANES))
    for u in range(NA_TILES_PER_STEP):
        t = pl.program_id(2) * NA_TILES_PER_STEP + u
        ws = jnp.clip(t * NA_TILE_ROWS - NA_KH // 2, 0, grid_rows - NA_WIN_ROWS)
        variant = jnp.where(t == 0, 0, jnp.where(t == n_tiles - 1, 2, 1))
        start = pl.multiple_of(ws * GRID_W, GRID_W)
        kwin = k_ref[pl.ds(start, NA_WIN_ROWS * GRID_W), :]
        vwin = v_ref[pl.ds(start, NA_WIN_ROWS * GRID_W), :]
        q = q_ref[u * tq:(u + 1) * tq, :]
        zero = jnp.zeros((), BF16)
        lhs = jnp.concatenate([jnp.where(lo, q, zero), jnp.where(lo, zero, q)], axis=0)
        bias = tab_ref[variant].reshape(2 * tq, NA_WIN_ROWS * GRID_W)
        o = _attend(lhs, [(kc_ref[...], vc_ref[...], None), (kwin, vwin, bias)])
        o_ref[u * tq:(u + 1) * tq, :] = jnp.where(lo, o[:tq], o[tq:]).astype(BF16)


def _na_attn(tab, q, k, v, kc, vc, batch, seq, ctx_len):
    tq = NA_TILE_ROWS * GRID_W
    n_tiles = seq // tq
    nk = NA_WIN_ROWS * GRID_W
    assert n_tiles % NA_TILES_PER_STEP == 0
    n_steps = n_tiles // NA_TILES_PER_STEP
    tq_step = tq * NA_TILES_PER_STEP
    return pl.pallas_call(
        functools.partial(_na_kernel, n_tiles),
        grid=(NA_HEADS // 2, batch, n_steps),
        in_specs=[
            pl.BlockSpec((3, 2, tq, nk), lambda j, b, t: (0, j, 0, 0)),
            pl.BlockSpec((tq_step, 128), lambda j, b, t: (b * n_steps + t, j)),
            pl.BlockSpec((seq, 128), lambda j, b, t: (b, j)),
            pl.BlockSpec((seq, 128), lambda j, b, t: (b, j)),
            pl.BlockSpec((ctx_len, 128), lambda j, b, t: (b, j)),
            pl.BlockSpec((ctx_len, 128), lambda j, b, t: (b, j)),
        ],
        out_specs=pl.BlockSpec((tq_step, 128), lambda j, b, t: (b * n_steps + t, j)),
        out_shape=jax.ShapeDtypeStruct((batch * seq, NA_HEADS * NA_DIM), BF16),
        compiler_params=_params(),
        name="na_attn",
    )(tab, q, k, v, kc, vc)


def _na_bias_table(rpb, grid_rows):
    cols = np.arange(GRID_W)
    col_start = np.clip(cols - NA_KW // 2, 0, GRID_W - NA_KW)
    col_ok = (cols[None, :] >= col_start[:, None]) & (cols[None, :] < col_start[:, None] + NA_KW)
    dc = np.clip(cols[None, :] - cols[:, None], -(NA_KW - 1), NA_KW - 1) + NA_KW - 1
    n_heads, n_dr = rpb.shape[0], rpb.shape[1]
    by_col = jnp.take(rpb.astype(F32) * LOG2E, jnp.asarray(dc.reshape(-1)), axis=2).reshape(n_heads, n_dr, GRID_W, GRID_W)
    by_col = jnp.where(jnp.asarray(col_ok)[None, None], by_col, NEG_INF)
    by_col = jnp.concatenate([by_col, jnp.full((n_heads, 1, GRID_W, GRID_W), NEG_INF, F32)], axis=1)
    n_tiles = grid_rows // NA_TILE_ROWS
    variants = []
    for t in (0, min(2, n_tiles - 1), n_tiles - 1):
        r0 = t * NA_TILE_ROWS
        ws = int(np.clip(r0 - NA_KH // 2, 0, grid_rows - NA_WIN_ROWS))
        r = r0 + np.arange(NA_TILE_ROWS)
        rs = np.clip(r - NA_KH // 2, 0, grid_rows - NA_KH)
        key_row = ws + np.arange(NA_WIN_ROWS)
        row_ok = (key_row[None, :] >= rs[:, None]) & (key_row[None, :] < rs[:, None] + NA_KH)
        dr = np.where(row_ok, key_row[None, :] - r[:, None] + NA_KH - 1, n_dr)
        rows = []
        for i in range(NA_TILE_ROWS):
            rows.append(jnp.concatenate([by_col[:, int(d)] for d in dr[i]], axis=-1))
        variants.append(jnp.concatenate(rows, axis=1))
    return jnp.stack(variants, axis=0)


def _post_kernel(a_ref, b_ref, x_ref, mod_ref, wa_ref, wb_ref, g1_ref, g2_ref, x1_ref, h_ref):
    y = _dot(a_ref[...], wa_ref[...]) + _dot(b_ref[...], wb_ref[...])
    x1 = x_ref[...] + mod_ref[0, 2:3, :] * _rms(y, g1_ref[...])
    x1_ref[...] = x1
    h_ref[...] = (_rms(x1, g2_ref[...]) * (1.0 + mod_ref[0, 4:5, :]) + mod_ref[0, 3:4, :]).astype(BF16)


def _post(rows, a, b, x, mod, wa, wb, g1, g2):
    n = rows.n
    return pl.pallas_call(
        _post_kernel,
        grid=rows.grid,
        in_specs=[
            rows.row_spec(a.shape[1]), rows.row_spec(b.shape[1]), rows.row_spec(D_MODEL), rows.mod_spec(),
            _const_spec(wa.shape), _const_spec(wb.shape), _const_spec((1, D_MODEL)), _const_spec((1, D_MODEL)),
        ],
        out_specs=[rows.row_spec(D_MODEL), rows.row_spec(D_MODEL)],
        out_shape=[jax.ShapeDtypeStruct((n, D_MODEL), F32), jax.ShapeDtypeStruct((n, D_MODEL), BF16)],
        compiler_params=_params(),
        name="post",
    )(a, b, x, mod, wa, wb, g1, g2)


def _gelu_tanh(x):
    return x * (0.5 * (1.0 + jnp.tanh(math.sqrt(2.0 / math.pi) * (x + 0.044715 * (x * x * x)))))


def _ffn_kernel(tm, tps, hp_ref, h_ref, hn_ref, x_ref, mod_ref, g_ref, wg_ref, wv_ref, cw_ref, wd_ref,
                o_ref, hh_ref, ug_ref, uv_ref, acc_ref):
    ti = pl.program_id(0) % tps
    hh_ref[0:HALO] = jnp.where(ti == 0, jnp.zeros((), BF16), hp_ref[...])
    hh_ref[HALO:HALO + tm] = h_ref[...]
    hh_ref[HALO + tm:] = jnp.where(ti == tps - 1, jnp.zeros((), BF16), hn_ref[...])
    acc_ref[...] = jnp.zeros_like(acc_ref)

    def up(c, slot):
        ug_ref[slot] = _dot(hh_ref[...], wg_ref[c])
        uv_ref[slot] = _dot(h_ref[...], wv_ref[c])

    def down(c, slot):
        cw = cw_ref[c]
        gate = cw[3:4]
        for tap in range(3):
            gate = gate + ug_ref[slot, HALO - 1 + tap:HALO - 1 + tap + tm] * cw[tap:tap + 1]
        act = (_gelu_tanh(gate) * uv_ref[slot]).astype(BF16)
        acc_ref[...] += _dot(act, wd_ref[c])

    up(0, 0)

    def step(i, carry):
        c = 2 * i
        up(c + 1, 1)
        down(c, 0)
        up(c + 2, 0)
        down(c + 1, 1)
        return carry

    assert N_FF_CHUNKS % 2 == 1
    lax.fori_loop(0, N_FF_CHUNKS // 2, step, 0)
    down(N_FF_CHUNKS - 1, 0)
    o_ref[...] = x_ref[...] + mod_ref[0, 5:6, :] * _rms(acc_ref[...], g_ref[...])


def _ffn(rows, h, x1, mod, g, wg, wv, cw, wd):
    n, tm, tps = rows.n, rows.tm, rows.tps
    per = tm // HALO
    last = n // HALO - 1
    return pl.pallas_call(
        functools.partial(_ffn_kernel, tm, tps),
        grid=rows.grid,
        in_specs=[
            pl.BlockSpec((HALO, D_MODEL), lambda i: (jnp.maximum(i * per - 1, 0), 0)),
            rows.row_spec(D_MODEL),
            pl.BlockSpec((HALO, D_MODEL), lambda i: (jnp.minimum((i + 1) * per, last), 0)),
            rows.row_spec(D_MODEL), rows.mod_spec(), _const_spec((1, D_MODEL)),
            _const_spec(wg.shape), _const_spec(wv.shape), _const_spec(cw.shape), _const_spec(wd.shape),
        ],
        out_specs=rows.row_spec(D_MODEL),
        out_shape=jax.ShapeDtypeStruct((n, D_MODEL), F32),
        scratch_shapes=[
            pltpu.VMEM((tm + 2 * HALO, D_MODEL), BF16),
            pltpu.VMEM((2, tm + 2 * HALO, FF_CHUNK), F32),
            pltpu.VMEM((2, tm, FF_CHUNK), F32),
            pltpu.VMEM((tm, D_MODEL), F32),
        ],
        compiler_params=_params(),
        name="ffn",
    )(h, h, h, x1, mod, g, wg, wv, cw, wd)


def _rope_angles(n, d):
    t = jnp.arange(n)
    rows = (t // GRID_W).astype(F32)
    cols = (t % GRID_W).astype(F32)
    freqs = ROPE_BASE ** (-jnp.arange(d // 4, dtype=F32) * 4.0 / d)
    ang = jnp.concatenate([rows[:, None] * freqs, cols[:, None] * freqs], axis=-1)
    return jnp.cos(ang), jnp.sin(ang)


def _rope_lane_tables(cos, sin, offset, group):
    n, half = cos.shape
    c = jnp.ones((n, group), F32)
    sa = jnp.zeros((n, group), F32)
    sb = jnp.zeros((n, group), F32)
    c = c.at[:, offset:offset + half].set(cos).at[:, offset + half:offset + 2 * half].set(cos)
    sa = sa.at[:, offset:offset + half].set(-sin)
    sb = sb.at[:, offset + half:offset + 2 * half].set(sin)
    reps = LANES // group
    return tuple(jnp.tile(a, (1, reps)) for a in (c, sa, sb))


def _identity_tables(n, width_blocks):
    ones = jnp.ones((n, LANES), F32)
    zeros = jnp.zeros((n, LANES), F32)
    return [ones, zeros, zeros][:width_blocks]


def _even_weights(w_in, w_out, w_uq, w_ukv):
    z = lambda r, c: jnp.zeros((r, c), F32)
    mla_in = MLA_Q_RANK + MLA_KV_RANK + MLA_ROPE
    kpe = w_in[:, MLA_Q_RANK + MLA_KV_RANK:mla_in]
    swa = w_in[:, mla_in:]
    swa_q = swa[:, :SWA_HEADS * SWA_DIM].reshape(D_MODEL, SWA_HEADS, SWA_DIM)
    pair_order = np.array([h for j in range(SWA_GROUP) for h in (j, SWA_GROUP + j)])
    swa_q = swa_q[:, pair_order, :].reshape(D_MODEL, SWA_HEADS * SWA_DIM)
    win = jnp.concatenate([
        w_in[:, :MLA_Q_RANK + MLA_KV_RANK], z(D_MODEL, 64), kpe, z(D_MODEL, 32), swa_q,
        swa[:, SWA_HEADS * SWA_DIM:]], axis=1)
    assert win.shape[1] == EVEN_COLS
    wuq = w_uq.reshape(MLA_Q_RANK, MLA_HEADS, MLA_NOPE + MLA_ROPE)
    wuq = jnp.pad(wuq, ((0, 0), (0, 0), (0, LANES - MLA_NOPE - MLA_ROPE))).reshape(MLA_Q_RANK, MLA_HEADS * LANES)
    wukv = w_ukv.reshape(MLA_KV_RANK, MLA_HEADS, MLA_NOPE + MLA_V)
    wuk = jnp.pad(wukv[:, :, :MLA_NOPE], ((0, 0), (0, 0), (0, LANES - MLA_NOPE))).reshape(MLA_KV_RANK, MLA_HEADS * LANES)
    wuv = wukv[:, :, MLA_NOPE:].reshape(MLA_KV_RANK, MLA_HEADS * MLA_V)
    n_a = MLA_HEADS * MLA_V
    wo_a = w_out[:n_a]
    wo_b = w_out[n_a:].reshape(SWA_HEADS, SWA_DIM, D_MODEL)[pair_order].reshape(SWA_HEADS * SWA_DIM, D_MODEL)
    return tuple(a.astype(BF16) for a in (win, wuq, wuk, wuv, wo_a, wo_b))


def _ffn_weights(w_up, conv_w, conv_b, w_down):
    def chunks(w):
        return w.reshape(D_MODEL, N_FF_CHUNKS, FF_CHUNK).transpose(1, 0, 2).astype(BF16)

    cw = jnp.concatenate([conv_w, conv_b[None], jnp.zeros((4, D_FF), F32)], axis=0)
    cw = cw.reshape(8, N_FF_CHUNKS, FF_CHUNK).transpose(1, 0, 2)
    wd = w_down.reshape(N_FF_CHUNKS, FF_CHUNK, D_MODEL).astype(BF16)
    return chunks(w_up[:, :D_FF]), chunks(w_up[:, D_FF:]), cw, wd


def kernel(x, c, ctx, c_ctx, ada_w, ada_b, pre_mix_g, post_mix_g, pre_ffn_g, post_ffn_g, ffn_up, ffn_conv_w,
           ffn_conv_b, ffn_down, even_w_in, even_w_out, mla_q_norm_g, mla_w_uq, mla_kv_norm_g, mla_w_ukv,
           swa_sink, odd_w_in, odd_w_out, na_rpb, gqa_q_g, gqa_k_g):
    batch, seq, _ = x.shape
    ctx_len = ctx.shape[1]
    depth = ada_w.shape[0]
    n_lat, n_ctx = batch * seq, batch * ctx_len

    pad_rows = -(batch + 1) % 8
    cc = jnp.concatenate([c, c_ctx[None], jnp.zeros((pad_rows, D_MODEL), F32)], axis=0)
    mods = _ada(cc, ada_w, ada_b).reshape(depth, cc.shape[0], 6, D_MODEL)
    mods = jnp.pad(mods, ((0, 0), (0, 0), (0, 2), (0, 0)))

    lat_rows = _Rows(n_lat, seq, 512, True)
    pre_rows = _Rows(n_lat, seq, 2 * PRE_HALF, True)
    ctx_rows = _Rows(n_ctx, ctx_len, ctx_len, False)

    cos_m, sin_m = _rope_angles(seq, MLA_ROPE)
    cos_s, sin_s = _rope_angles(seq, SWA_DIM)
    cos_g, sin_g = _rope_angles(seq, GQA_DIM)
    gc, gsa, gsb = _rope_lane_tables(cos_g, sin_g, 0, LANES)
    tab_even = jnp.concatenate(_rope_lane_tables(cos_m, sin_m, MLA_NOPE, LANES)
                               + _rope_lane_tables(cos_s, sin_s, 0, SWA_DIM), axis=1)
    tab_odd = jnp.concatenate([gc, gsa + gsb], axis=1)
    tab_even_ctx = jnp.concatenate(_identity_tables(ctx_len, 3) * 2, axis=1)
    tab_odd_ctx = jnp.concatenate(_identity_tables(ctx_len, 2), axis=1)

    x = x.reshape(n_lat, D_MODEL)
    ctx = ctx.reshape(n_ctx, D_MODEL)
    row = lambda v: v.reshape(1, -1)

    for layer in range(depth):
        need_ctx = layer < depth - 1
        i = layer // 2
        mod_lat = mods[layer, :batch]
        mod_ctx = mods[layer, batch:batch + 1]
        if layer % 2 == 0:
            win, wuq, wuk, wuv, wo_a, wo_b = _even_weights(even_w_in[i], even_w_out[i], mla_w_uq[i], mla_w_ukv[i])
            shared = (row(pre_mix_g[layer]), win, row(mla_q_norm_g[i]), wuq, row(mla_kv_norm_g[i]), wuk, wuv)
            qm, km, vm, qs, ks, vs = _pre_even(pre_rows, x, mod_lat, *shared, tab_even)
            qmc, kmc, vmc, qsc, ksc, vsc = _pre_even(ctx_rows, ctx, mod_ctx, *shared, tab_even_ctx)
            a_lat = _mla_attn(qm, kmc, vmc, km, vm, batch, seq, ctx_len, seq, 1024)
            b_lat = _swa_attn_lat(swa_sink[i], qs, ks, vs, ksc, vsc, batch, seq, ctx_len, 512)
            if need_ctx:
                a_ctx = _mla_attn(qmc, kmc, vmc, None, None, batch, ctx_len, ctx_len, seq, ctx_len)
                b_ctx = _swa_attn_ctx(swa_sink[i], qsc, ksc, vsc, batch, ctx_len)
        else:
            win = odd_w_in[i].astype(BF16)
            n_a = NA_HEADS * NA_DIM
            wo_a, wo_b = odd_w_out[i][:n_a].astype(BF16), odd_w_out[i][n_a:].astype(BF16)
            shared = (row(pre_mix_g[layer]), win, row(gqa_q_g[i]), row(gqa_k_g[i]))
            qn, kn, vn, qq, kq, vq = _pre_odd(pre_rows, x, mod_lat, *shared, tab_odd)
            qnc, knc, vnc, qqc, kqc, vqc = _pre_odd(ctx_rows, ctx, mod_ctx, *shared, tab_odd_ctx)
            na_tab = _na_bias_table(na_rpb[i], seq // GRID_W)
            a_lat = _na_attn(na_tab, qn, kn, vn, knc, vnc, batch, seq, ctx_len)
            b_lat = _gqa_attn(qq, kqc, vqc, kq, vq, batch, seq, ctx_len, 512)
            assert not need_ctx, "context-query attention for odd layers is only needed when depth > 2"
        wg, wv, cw, wd = _ffn_weights(ffn_up[layer], ffn_conv_w[layer], ffn_conv_b[layer], ffn_down[layer])
        g1, g2, g3 = row(post_mix_g[layer]), row(pre_ffn_g[layer]), row(post_ffn_g[layer])
        x1, h = _post(lat_rows, a_lat, b_lat, x, mod_lat, wo_a, wo_b, g1, g2)
        x = _ffn(lat_rows, h, x1, mod_lat, g3, wg, wv, cw, wd)
        if need_ctx:
            c1, hc = _post(ctx_rows, a_ctx, b_ctx, ctx, mod_ctx, wo_a, wo_b, g1, g2)
            ctx = _ffn(ctx_rows, hc, c1, mod_ctx, g3, wg, wv, cw, wd)
    return x.reshape(batch, seq, D_MODEL)
```

```python
import functools
import math

import jax
import jax.numpy as jnp
import numpy as np
from jax import lax
from jax.experimental import pallas as pl
from jax.experimental.pallas import tpu as pltpu

F32 = jnp.float32
BF16 = jnp.bfloat16

D_MODEL = 1024
GRID_W = 64
ROPE_BASE = 10000.0
EPS = 1e-6
NEG_INF = -1e30
LOG2E = math.log2(math.e)
KEY_CHUNK = 512
LANES = 128

MLA_HEADS, MLA_Q_RANK, MLA_KV_RANK, MLA_NOPE, MLA_ROPE, MLA_V = 8, 256, 128, 64, 32, 64
SWA_HEADS, SWA_KV_HEADS, SWA_DIM, SWA_WINDOW = 8, 2, 64, 128
NA_HEADS, NA_DIM, NA_KH, NA_KW = 8, 64, 8, 16
GQA_HEADS, GQA_KV_HEADS, GQA_DIM = 4, 2, 128
D_FF = 2816
FF_CHUNK = 256
N_FF_CHUNKS = D_FF // FF_CHUNK
HALO = 16

NA_TILE_ROWS = 4
NA_WIN_ROWS = NA_TILE_ROWS + NA_KH - 1

VMEM_LIMIT = 56 * 1024 * 1024


def _params():
    return pltpu.CompilerParams(vmem_limit_bytes=VMEM_LIMIT)


def _dot(a, b):
    return jnp.dot(a, b, preferred_element_type=F32)


def _dot_nt(a, b):
    return lax.dot_general(a, b, (((1,), (1,)), ((), ())), preferred_element_type=F32)


def _rms(x, g):
    return x * lax.rsqrt(jnp.mean(x * x, axis=-1, keepdims=True) + EPS) * g


def _rope(x, c, sa, sb, sha, shb):
    return x * c + pltpu.roll(x, sha, axis=1) * sa + pltpu.roll(x, shb, axis=1) * sb


def _lane_lo(shape):
    return lax.broadcasted_iota(jnp.int32, shape, len(shape) - 1) < (LANES // 2)


def _const_spec(shape):
    nd = len(shape)
    return pl.BlockSpec(shape, lambda *_: (0,) * nd, pipeline_mode=pl.Buffered(1))


def _ada_kernel(cc_ref, w_ref, b_ref, o_ref):
    cc = cc_ref[...]
    s = cc * (1.0 / (1.0 + jnp.exp(-cc)))
    o_ref[0] = _dot(s.astype(BF16), w_ref[0].astype(BF16)) + b_ref[0]


def _ada(cc, ada_w, ada_b):
    depth = ada_w.shape[0]
    rows = cc.shape[0]
    return pl.pallas_call(
        _ada_kernel,
        grid=(depth, 6),
        in_specs=[
            pl.BlockSpec((rows, D_MODEL), lambda l, j: (0, 0)),
            pl.BlockSpec((1, D_MODEL, D_MODEL), lambda l, j: (l, 0, j)),
            pl.BlockSpec((1, 1, D_MODEL), lambda l, j: (l, 0, j)),
        ],
        out_specs=pl.BlockSpec((1, rows, D_MODEL), lambda l, j: (l, 0, j)),
        out_shape=jax.ShapeDtypeStruct((depth, rows, 6 * D_MODEL), F32),
        compiler_params=_params(),
        name="ada",
    )(cc, ada_w, ada_b.reshape(depth, 1, 6 * D_MODEL))


class _Rows:
    def __init__(self, n, seq, tm, per_batch_mod):
        assert seq % tm == 0 and n % seq == 0
        self.n, self.seq, self.tm = n, seq, tm
        self.tps = seq // tm
        self.grid = (n // tm,)
        self.per_batch_mod = per_batch_mod

    def row_spec(self, width):
        return pl.BlockSpec((self.tm, width), lambda i: (i, 0))

    def mod_spec(self):
        if self.per_batch_mod:
            tps = self.tps
            return pl.BlockSpec((1, 8, D_MODEL), lambda i: (i // tps, 0, 0))
        return pl.BlockSpec((1, 8, D_MODEL), lambda i: (0, 0, 0))

    def table_spec(self, width):
        tps = self.tps
        return pl.BlockSpec((self.tm, width), lambda i: (i % tps, 0))


PRE_HALF = 512


def _row_halves(tm):
    half = min(tm, PRE_HALF)
    assert tm % half == 0
    return [slice(r, r + half) for r in range(0, tm, half)]


def _modulated(x_ref, mod_ref, g_ref, rows, shift_row, scale_row):
    x = x_ref[rows, :]
    shift = mod_ref[0, shift_row:shift_row + 1, :]
    scale = mod_ref[0, scale_row:scale_row + 1, :]
    return _rms(x, g_ref[...]) * (1.0 + scale) + shift


EVEN_COLS = 1280


def _pre_even_kernel(x_ref, mod_ref, g_ref, win_ref, qg_ref, wuq_ref, kvg_ref, wuk_ref, wuv_ref,
                     tab_ref, qm_ref, km_ref, vm_ref, qs_ref, ks_ref, vs_ref):
    mla_scale = LOG2E * (MLA_NOPE + MLA_ROPE) ** -0.5
    swa_scale = LOG2E * SWA_DIM ** -0.5
    for rows in _row_halves(x_ref.shape[0]):
        h = _modulated(x_ref, mod_ref, g_ref, rows, 0, 1).astype(BF16)
        mc, ma, mb = tab_ref[rows, 0:128], tab_ref[rows, 128:256], tab_ref[rows, 256:384]
        sc, sa, sb = tab_ref[rows, 384:512], tab_ref[rows, 512:640], tab_ref[rows, 640:768]

        pa = _dot(h, win_ref[:, 0:512])
        cq = _rms(pa[:, 0:256], qg_ref[...]).astype(BF16)
        ckv = _rms(pa[:, 256:384], kvg_ref[...]).astype(BF16)
        kpe = _rope(pa[:, 384:512], mc, ma, mb, 112, 16)
        q = _dot(cq, wuq_ref[...])
        kn = _dot(ckv, wuk_ref[...])
        for j in range(MLA_HEADS):
            sl = slice(j * LANES, (j + 1) * LANES)
            qm_ref[rows, sl] = (_rope(q[:, sl], mc, ma, mb, 112, 16) * mla_scale).astype(BF16)
            km_ref[rows, sl] = (kn[:, sl] + kpe).astype(BF16)
        vm_ref[rows, :] = _dot(ckv, wuv_ref[...]).astype(BF16)

        pq = _dot(h, win_ref[:, 512:1024])
        for j in range(4):
            sl = slice(j * LANES, (j + 1) * LANES)
            qs_ref[rows, sl] = (_rope(pq[:, sl], sc, sa, sb, 96, 32) * swa_scale).astype(BF16)
        pkv = _dot(h, win_ref[:, 1024:1280])
        ks_ref[rows, :] = _rope(pkv[:, 0:128], sc, sa, sb, 96, 32).astype(BF16)
        vs_ref[rows, :] = pkv[:, 128:256].astype(BF16)


def _pre_even(rows, x, mod, g, win, qg, wuq, kvg, wuk, wuv, tab):
    n = rows.n
    widths = (1024, 1024, 512, 512, 128, 128)
    return pl.pallas_call(
        _pre_even_kernel,
        grid=rows.grid,
        in_specs=[
            rows.row_spec(D_MODEL), rows.mod_spec(), _const_spec((1, D_MODEL)),
            _const_spec(win.shape), _const_spec(qg.shape), _const_spec(wuq.shape),
            _const_spec(kvg.shape), _const_spec(wuk.shape), _const_spec(wuv.shape),
            rows.table_spec(768),
        ],
        out_specs=[rows.row_spec(w) for w in widths],
        out_shape=[jax.ShapeDtypeStruct((n, w), BF16) for w in widths],
        compiler_params=_params(),
        name="pre_even",
    )(x, mod, g, win, qg, wuq, kvg, wuk, wuv, tab)


def _pre_odd_kernel(x_ref, mod_ref, g_ref, win_ref, qg_ref, kg_ref, tab_ref,
                    qn_ref, kn_ref, vn_ref, qq_ref, kq_ref, vq_ref):
    na_scale = LOG2E * NA_DIM ** -0.5
    gqa_scale = LOG2E * GQA_DIM ** -0.5
    for rows in _row_halves(x_ref.shape[0]):
        h = _modulated(x_ref, mod_ref, g_ref, rows, 0, 1).astype(BF16)
        gc, gs = tab_ref[rows, 0:128], tab_ref[rows, 128:256]

        def normed(blk, gain, gc=gc, gs=gs):
            y = _rms(blk, gain)
            return y * gc + pltpu.roll(y, 64, axis=1) * gs

        qn_ref[rows, :] = (_dot(h, win_ref[:, 0:512]) * na_scale).astype(BF16)
        kn_ref[rows, :] = _dot(h, win_ref[:, 512:1024]).astype(BF16)
        vn_ref[rows, :] = _dot(h, win_ref[:, 1024:1536]).astype(BF16)
        pq = _dot(h, win_ref[:, 1536:2048])
        for j in range(GQA_HEADS):
            sl = slice(j * LANES, (j + 1) * LANES)
            qq_ref[rows, sl] = (normed(pq[:, sl], qg_ref[...]) * gqa_scale).astype(BF16)
        pkv = _dot(h, win_ref[:, 2048:2560])
        for j in range(GQA_KV_HEADS):
            sl = slice(j * LANES, (j + 1) * LANES)
            kq_ref[rows, sl] = normed(pkv[:, sl], kg_ref[...]).astype(BF16)
        vq_ref[rows, :] = pkv[:, 256:512].astype(BF16)


def _pre_odd(rows, x, mod, g, win, qg, kg, tab):
    n = rows.n
    widths = (512, 512, 512, 512, 256, 256)
    return pl.pallas_call(
        _pre_odd_kernel,
        grid=rows.grid,
        in_specs=[
            rows.row_spec(D_MODEL), rows.mod_spec(), _const_spec((1, D_MODEL)),
            _const_spec(win.shape), _const_spec(qg.shape), _const_spec(kg.shape),
            rows.table_spec(256),
        ],
        out_specs=[rows.row_spec(w) for w in widths],
        out_shape=[jax.ShapeDtypeStruct((n, w), BF16) for w in widths],
        compiler_params=_params(),
        name="pre_odd",
    )(x, mod, g, win, qg, kg, tab)


def _attend(q, chunks, sink=None):
    rows = q.shape[0]
    if sink is None:
        m, acc = None, None
    else:
        m = sink
        acc = jnp.concatenate([jnp.zeros((rows, LANES), F32), jnp.ones((rows, LANES), F32)], axis=1)
    for k, v, bias in chunks:
        s = _dot_nt(q, k)
        if bias is not None:
            s = s + bias
        cm = jnp.max(s, axis=-1, keepdims=True)
        m_new = cm if m is None else jnp.maximum(m, cm)
        p = jnp.exp2(s - m_new).astype(BF16)
        vx = jnp.concatenate([v, jnp.ones(v.shape, BF16)], axis=1)
        pv = _dot(p, vx)
        acc = pv if acc is None else jnp.exp2(m - m_new) * acc + pv
        m = m_new
    return acc[:, :LANES] / acc[:, LANES:]


def _chunks(k_ref, v_ref, lanes, size):
    n = k_ref.shape[0]
    assert n % size == 0
    return [(k_ref[c * size:(c + 1) * size, lanes], v_ref[c * size:(c + 1) * size, :], None)
            for c in range(n // size)]


def _mla_kernel(has_lat, q_ref, kc_ref, vc_ref, *rest):
    if has_lat:
        kl_ref, vl_ref, o_ref = rest
    else:
        (o_ref,) = rest
    outs = []
    for e in range(2):
        sl = slice(e * LANES, (e + 1) * LANES)
        chunks = _chunks(kc_ref, vc_ref, sl, kc_ref.shape[0])
        if has_lat:
            chunks += _chunks(kl_ref, vl_ref, sl, KEY_CHUNK)
        outs.append(_attend(q_ref[:, sl], chunks))
    o_ref[...] = jnp.where(_lane_lo(outs[0].shape), outs[0], outs[1]).astype(BF16)


def _mla_attn(q, kc, vc, kl, vl, batch, q_seq, ctx_len, lat_seq, tq):
    has_lat = kl is not None
    nq = q_seq // tq
    in_specs = [
        pl.BlockSpec((tq, 256), lambda b, j, t: (b * nq + t, j)),
        pl.BlockSpec((ctx_len, 256), lambda b, j, t: (b, j)),
        pl.BlockSpec((ctx_len, 128), lambda b, j, t: (b, j)),
    ]
    args = [q, kc, vc]
    if has_lat:
        in_specs += [
            pl.BlockSpec((lat_seq, 256), lambda b, j, t: (b, j)),
            pl.BlockSpec((lat_seq, 128), lambda b, j, t: (b, j)),
        ]
        args += [kl, vl]
    return pl.pallas_call(
        functools.partial(_mla_kernel, has_lat),
        grid=(batch, MLA_HEADS // 2, nq),
        in_specs=in_specs,
        out_specs=pl.BlockSpec((tq, 128), lambda b, j, t: (b * nq + t, j)),
        out_shape=jax.ShapeDtypeStruct((batch * q_seq, MLA_HEADS * MLA_V), BF16),
        compiler_params=_params(),
        name="mla_attn_lat" if has_lat else "mla_attn_ctx",
    )(*args)


def _gqa_kernel(q_ref, kc_ref, vc_ref, kl_ref, vl_ref, o_ref):
    tq = q_ref.shape[0]
    per_group = GQA_HEADS // GQA_KV_HEADS
    q = jnp.concatenate([q_ref[:, e * LANES:(e + 1) * LANES] for e in range(per_group)], axis=0)
    every = slice(None)
    chunks = _chunks(kc_ref, vc_ref, every, kc_ref.shape[0]) + _chunks(kl_ref, vl_ref, every, KEY_CHUNK)
    o = _attend(q, chunks).astype(BF16)
    for e in range(per_group):
        o_ref[:, e * LANES:(e + 1) * LANES] = o[e * tq:(e + 1) * tq]


def _gqa_attn(q, kc, vc, kl, vl, batch, seq, ctx_len, tq):
    nq = seq // tq
    return pl.pallas_call(
        _gqa_kernel,
        grid=(batch, GQA_KV_HEADS, nq),
        in_specs=[
            pl.BlockSpec((tq, 256), lambda b, g, t: (b * nq + t, g)),
            pl.BlockSpec((ctx_len, 128), lambda b, g, t: (b, g)),
            pl.BlockSpec((ctx_len, 128), lambda b, g, t: (b, g)),
            pl.BlockSpec((seq, 128), lambda b, g, t: (b, g)),
            pl.BlockSpec((seq, 128), lambda b, g, t: (b, g)),
        ],
        out_specs=pl.BlockSpec((tq, 256), lambda b, g, t: (b * nq + t, g)),
        out_shape=jax.ShapeDtypeStruct((batch * seq, GQA_HEADS * GQA_DIM), BF16),
        compiler_params=_params(),
        name="gqa_attn",
    )(q, kc, vc, kl, vl)


SWA_QB = 128
SWA_GROUP = SWA_HEADS // SWA_KV_HEADS


def _swa_group_lhs(q_ref, r0, rows, g):
    lo = _lane_lo((rows, LANES))
    keep = lo if g == 0 else jnp.logical_not(lo)
    blocks = [jnp.where(keep, q_ref[r0:r0 + rows, j * LANES:(j + 1) * LANES], jnp.zeros((), BF16))
              for j in range(SWA_GROUP)]
    return jnp.concatenate(blocks, axis=0)


def _swa_sink_col(sink_ref, g, rows):
    blk = lax.broadcasted_iota(jnp.int32, (SWA_GROUP * rows, 1), 0) // rows
    col = jnp.full((SWA_GROUP * rows, 1), sink_ref[g * SWA_GROUP], F32)
    for j in range(1, SWA_GROUP):
        col = jnp.where(blk == j, sink_ref[g * SWA_GROUP + j], col)
    return col * LOG2E


def _swa_store(o_ref, r0, rows, o_groups):
    lo = _lane_lo((rows, LANES))
    for j in range(SWA_GROUP):
        a = o_groups[0][j * rows:(j + 1) * rows]
        b = o_groups[1][j * rows:(j + 1) * rows]
        o_ref[r0:r0 + rows, j * LANES:(j + 1) * LANES] = jnp.where(lo, a, b).astype(BF16)


SWA_BAND = SWA_QB + 2 * SWA_WINDOW


def _swa_mask_table():
    ii = np.arange(SWA_GROUP * SWA_QB)[:, None] % SWA_QB
    jj = np.arange(SWA_BAND)[None, :]
    in_window = np.abs(jj - SWA_WINDOW - ii) <= SWA_WINDOW
    first = in_window & (jj >= SWA_WINDOW)
    last = in_window & (jj < SWA_WINDOW + SWA_QB)
    return jnp.asarray(np.where(np.stack([first, in_window, last]), 0.0, NEG_INF), F32)


def _swa_lat_kernel(tq, sink_ref, mask_ref, q_ref, kp_ref, kt_ref, kn_ref, vp_ref, vt_ref, vn_ref,
                    kc_ref, vc_ref, o_ref):
    t = pl.program_id(1)
    kband = jnp.concatenate([kp_ref[...], kt_ref[...], kn_ref[...]], axis=0)
    vband = jnp.concatenate([vp_ref[...], vt_ref[...], vn_ref[...]], axis=0)
    band = SWA_BAND
    n_blocks = tq // SWA_QB
    for i in range(n_blocks):
        variant = 1
        if i == 0:
            variant = jnp.where(t == 0, 0, variant)
        if i == n_blocks - 1:
            variant = jnp.where(t == pl.num_programs(1) - 1, 2, variant)
        bias = mask_ref[variant]
        kb = kband[i * SWA_QB:i * SWA_QB + band]
        vb = vband[i * SWA_QB:i * SWA_QB + band]
        o_groups = []
        for g in range(SWA_KV_HEADS):
            lhs = _swa_group_lhs(q_ref, i * SWA_QB, SWA_QB, g)
            o_groups.append(_attend(lhs, [(kc_ref[...], vc_ref[...], None), (kb, vb, bias)],
                                    _swa_sink_col(sink_ref, g, SWA_QB)))
        _swa_store(o_ref, i * SWA_QB, SWA_QB, o_groups)


def _swa_ctx_kernel(ctx_len, sink_ref, q_ref, kc_ref, vc_ref, o_ref):
    o_groups = []
    for g in range(SWA_KV_HEADS):
        lhs = _swa_group_lhs(q_ref, 0, ctx_len, g)
        o_groups.append(_attend(lhs, [(kc_ref[...], vc_ref[...], None)], _swa_sink_col(sink_ref, g, ctx_len)))
    _swa_store(o_ref, 0, ctx_len, o_groups)


def _swa_attn_lat(sink, q, k, v, kc, vc, batch, seq, ctx_len, tq):
    nq = seq // tq
    nb = seq // SWA_WINDOW
    per = tq // SWA_WINDOW
    prev_spec = pl.BlockSpec((SWA_WINDOW, 128), lambda b, t: (b * nb + jnp.maximum(t * per - 1, 0), 0))
    next_spec = pl.BlockSpec((SWA_WINDOW, 128), lambda b, t: (b * nb + jnp.minimum(t * per + per, nb - 1), 0))
    tile_spec = pl.BlockSpec((tq, 128), lambda b, t: (b * nq + t, 0))
    ctx_spec = pl.BlockSpec((ctx_len, 128), lambda b, t: (b, 0))
    mask = _swa_mask_table()
    return pl.pallas_call(
        functools.partial(_swa_lat_kernel, tq),
        grid=(batch, nq),
        in_specs=[
            pl.BlockSpec(memory_space=pltpu.SMEM),
            pl.BlockSpec(mask.shape, lambda b, t: (0, 0, 0)),
            pl.BlockSpec((tq, 512), lambda b, t: (b * nq + t, 0)),
            prev_spec, tile_spec, next_spec, prev_spec, tile_spec, next_spec,
            ctx_spec, ctx_spec,
        ],
        out_specs=pl.BlockSpec((tq, 512), lambda b, t: (b * nq + t, 0)),
        out_shape=jax.ShapeDtypeStruct((batch * seq, SWA_HEADS * SWA_DIM), BF16),
        compiler_params=_params(),
        name="swa_attn_lat",
    )(sink, mask, q, k, k, k, v, v, v, kc, vc)


def _swa_attn_ctx(sink, q, kc, vc, batch, ctx_len):
    return pl.pallas_call(
        functools.partial(_swa_ctx_kernel, ctx_len),
        grid=(batch,),
        in_specs=[
            pl.BlockSpec(memory_space=pltpu.SMEM),
            pl.BlockSpec((ctx_len, 512), lambda b: (b, 0)),
            pl.BlockSpec((ctx_len, 128), lambda b: (b, 0)),
            pl.BlockSpec((ctx_len, 128), lambda b: (b, 0)),
        ],
        out_specs=pl.BlockSpec((ctx_len, 512), lambda b: (b, 0)),
        out_shape=jax.ShapeDtypeStruct((batch * ctx_len, SWA_HEADS * SWA_DIM), BF16),
        compiler_params=_params(),
        name="swa_attn_ctx",
    )(sink, q, kc, vc)


NA_TILES_PER_STEP = 8


def _na_kernel(n_tiles, tab_ref, q_ref, k_ref, v_ref, kc_ref, vc_ref, o_ref):
    grid_rows = n_tiles * NA_TILE_ROWS
    tq = NA_TILE_ROWS * GRID_W
    lo = _lane_lo((tq, LANES))
    for u in range(NA_TILES_PER_STEP):
        t = pl.program_id(2) * NA_TILES_PER_STEP + u
        ws = jnp.clip(t * NA_TILE_ROWS - NA_KH // 2, 0, grid_rows - NA_WIN_ROWS)
        variant = jnp.where(t == 0, 0, jnp.where(t == n_tiles - 1, 2, 1))
        start = pl.multiple_of(ws * GRID_W, GRID_W)
        kwin = k_ref[pl.ds(start, NA_WIN_ROWS * GRID_W), :]
        vwin = v_ref[pl.ds(start, NA_WIN_ROWS * GRID_W), :]
        q = q_ref[u * tq:(u + 1) * tq, :]
        zero = jnp.zeros((), BF16)
        lhs = jnp.concatenate([jnp.where(lo, q, zero), jnp.where(lo, zero, q)], axis=0)
        bias = tab_ref[variant].reshape(2 * tq, NA_WIN_ROWS * GRID_W)
        o = _attend(lhs, [(kc_ref[...], vc_ref[...], None), (kwin, vwin, bias)])
        o_ref[u * tq:(u + 1) * tq, :] = jnp.where(lo, o[:tq], o[tq:]).astype(BF16)


def _na_attn(tab, q, k, v, kc, vc, batch, seq, ctx_len):
    tq = NA_TILE_ROWS * GRID_W
    n_tiles = seq // tq
    nk = NA_WIN_ROWS * GRID_W
    assert n_tiles % NA_TILES_PER_STEP == 0
    n_steps = n_tiles // NA_TILES_PER_STEP
    tq_step = tq * NA_TILES_PER_STEP
    return pl.pallas_call(
        functools.partial(_na_kernel, n_tiles),
        grid=(NA_HEADS // 2, batch, n_steps),
        in_specs=[
            pl.BlockSpec((3, 2, tq, nk), lambda j, b, t: (0, j, 0, 0)),
            pl.BlockSpec((tq_step, 128), lambda j, b, t: (b * n_steps + t, j)),
            pl.BlockSpec((seq, 128), lambda j, b, t: (b, j)),
            pl.BlockSpec((seq, 128), lambda j, b, t: (b, j)),
            pl.BlockSpec((ctx_len, 128), lambda j, b, t: (b, j)),
            pl.BlockSpec((ctx_len, 128), lambda j, b, t: (b, j)),
        ],
        out_specs=pl.BlockSpec((tq_step, 128), lambda j, b, t: (b * n_steps + t, j)),
        out_shape=jax.ShapeDtypeStruct((batch * seq, NA_HEADS * NA_DIM), BF16),
        compiler_params=_params(),
        name="na_attn",
    )(tab, q, k, v, kc, vc)


def _na_bias_table(rpb, grid_rows):
    cols = np.arange(GRID_W)
    col_start = np.clip(cols - NA_KW // 2, 0, GRID_W - NA_KW)
    col_ok = (cols[None, :] >= col_start[:, None]) & (cols[None, :] < col_start[:, None] + NA_KW)
    dc = np.clip(cols[None, :] - cols[:, None], -(NA_KW - 1), NA_KW - 1) + NA_KW - 1
    n_heads, n_dr = rpb.shape[0], rpb.shape[1]
    by_col = jnp.take(rpb.astype(F32) * LOG2E, jnp.asarray(dc.reshape(-1)), axis=2).reshape(n_heads, n_dr, GRID_W, GRID_W)
    by_col = jnp.where(jnp.asarray(col_ok)[None, None], by_col, NEG_INF)
    by_col = jnp.concatenate([by_col, jnp.full((n_heads, 1, GRID_W, GRID_W), NEG_INF, F32)], axis=1)
    n_tiles = grid_rows // NA_TILE_ROWS
    variants = []
    for t in (0, min(2, n_tiles - 1), n_tiles - 1):
        r0 = t * NA_TILE_ROWS
        ws = int(np.clip(r0 - NA_KH // 2, 0, grid_rows - NA_WIN_ROWS))
        r = r0 + np.arange(NA_TILE_ROWS)
        rs = np.clip(r - NA_KH // 2, 0, grid_rows - NA_KH)
        key_row = ws + np.arange(NA_WIN_ROWS)
        row_ok = (key_row[None, :] >= rs[:, None]) & (key_row[None, :] < rs[:, None] + NA_KH)
        dr = np.where(row_ok, key_row[None, :] - r[:, None] + NA_KH - 1, n_dr)
        rows = []
        for i in range(NA_TILE_ROWS):
            rows.append(jnp.concatenate([by_col[:, int(d)] for d in dr[i]], axis=-1))
        variants.append(jnp.concatenate(rows, axis=1))
    return jnp.stack(variants, axis=0)


def _gelu_tanh(x):
    return x * (0.5 * (1.0 + jnp.tanh(math.sqrt(2.0 / math.pi) * (x + 0.044715 * (x * x * x)))))


def _mix_ffn_kernel(tm, tps, a_ref, b_ref, x_ref, modp_ref, modf_ref, wa_ref, wb_ref, g1_ref, g2_ref, g3_ref,
                    wg_ref, wv_ref, cw_ref, wd_ref, o_ref, hh_ref, x1_ref, ug_ref, uv_ref, acc_ref):
    i = pl.program_id(0)
    zero = jnp.zeros((), BF16)

    @pl.when(i == 0)
    def _():
        hh_ref[...] = jnp.zeros_like(hh_ref)
        x1_ref[...] = jnp.zeros_like(x1_ref)

    def step(s):
        o = 1 - s
        y = _dot(a_ref[...], wa_ref[...]) + _dot(b_ref[...], wb_ref[...])
        x1 = x_ref[...] + modp_ref[0, 2:3, :] * _rms(y, g1_ref[...])
        h = (_rms(x1, g2_ref[...]) * (1.0 + modp_ref[0, 4:5, :]) + modp_ref[0, 3:4, :]).astype(BF16)
        x1_ref[s] = x1
        hh_ref[s, HALO:HALO + tm] = h
        first = i % tps == 0
        hh_ref[s, 0:HALO] = jnp.where(first, zero, hh_ref[o, tm:tm + HALO])
        hh_ref[o, HALO + tm:] = jnp.where(first, zero, h[0:HALO])

        acc_ref[...] = jnp.zeros_like(acc_ref)

        def up(c, slot):
            uv_ref[slot] = _dot(hh_ref[o, HALO:HALO + tm], wv_ref[c])
            ug_ref[slot] = _dot(hh_ref[o], wg_ref[c])

        def down(c, slot):
            cw = cw_ref[c]
            gate = cw[3:4]
            for tap in range(3):
                gate = gate + ug_ref[slot, HALO - 1 + tap:HALO - 1 + tap + tm] * cw[tap:tap + 1]
            act = (_gelu_tanh(gate) * uv_ref[slot]).astype(BF16)
            acc_ref[...] += _dot(act, wd_ref[c])

        up(0, 0)

        def trip(t, carry):
            c = 2 * t
            up(c + 1, 1)
            down(c, 0)
            up(c + 2, 0)
            down(c + 1, 1)
            return carry

        assert N_FF_CHUNKS % 2 == 1
        lax.fori_loop(0, N_FF_CHUNKS // 2, trip, 0)
        down(N_FF_CHUNKS - 1, 0)
        o_ref[...] = x1_ref[o] + modf_ref[0, 5:6, :] * _rms(acc_ref[...], g3_ref[...])

    pl.when(i % 2 == 0)(lambda: step(0))
    pl.when(i % 2 == 1)(lambda: step(1))


def _mix_ffn(rows, a, b, x, mod, wa, wb, g1, g2, g3, wg, wv, cw, wd):
    n, tm, tps = rows.n, rows.tm, rows.tps
    n_tiles = n // tm
    assert n_tiles % tps == 0

    def tile_in(width):
        return pl.BlockSpec((tm, width), lambda i: (jnp.minimum(i, n_tiles - 1), 0))

    def mod_spec(tile_of_step):
        if rows.per_batch_mod:
            return pl.BlockSpec((1, 8, D_MODEL), lambda i: (tile_of_step(i) // tps, 0, 0))
        return pl.BlockSpec((1, 8, D_MODEL), lambda i: (0, 0, 0))

    row = _const_spec((1, D_MODEL))
    return pl.pallas_call(
        functools.partial(_mix_ffn_kernel, tm, tps),
        grid=(n_tiles + 1,),
        in_specs=[
            tile_in(a.shape[1]), tile_in(b.shape[1]), tile_in(D_MODEL),
            mod_spec(lambda i: jnp.minimum(i, n_tiles - 1)), mod_spec(lambda i: jnp.maximum(i - 1, 0)),
            _const_spec(wa.shape), _const_spec(wb.shape), row, row, row,
            _const_spec(wg.shape), _const_spec(wv.shape), _const_spec(cw.shape), _const_spec(wd.shape),
        ],
        out_specs=pl.BlockSpec((tm, D_MODEL), lambda i: (jnp.maximum(i - 1, 0), 0)),
        out_shape=jax.ShapeDtypeStruct((n, D_MODEL), F32),
        scratch_shapes=[
            pltpu.VMEM((2, tm + 2 * HALO, D_MODEL), BF16),
            pltpu.VMEM((2, tm, D_MODEL), F32),
            pltpu.VMEM((2, tm + 2 * HALO, FF_CHUNK), F32),
            pltpu.VMEM((2, tm, FF_CHUNK), F32),
            pltpu.VMEM((tm, D_MODEL), F32),
        ],
        compiler_params=_params(),
        name="mix_ffn",
    )(a, b, x, mod, mod, wa, wb, g1, g2, g3, wg, wv, cw, wd)


def _rope_angles(n, d):
    t = jnp.arange(n)
    rows = (t // GRID_W).astype(F32)
    cols = (t % GRID_W).astype(F32)
    freqs = ROPE_BASE ** (-jnp.arange(d // 4, dtype=F32) * 4.0 / d)
    ang = jnp.concatenate([rows[:, None] * freqs, cols[:, None] * freqs], axis=-1)
    return jnp.cos(ang), jnp.sin(ang)


def _rope_lane_tables(cos, sin, offset, group):
    n, half = cos.shape
    c = jnp.ones((n, group), F32)
    sa = jnp.zeros((n, group), F32)
    sb = jnp.zeros((n, group), F32)
    c = c.at[:, offset:offset + half].set(cos).at[:, offset + half:offset + 2 * half].set(cos)
    sa = sa.at[:, offset:offset + half].set(-sin)
    sb = sb.at[:, offset + half:offset + 2 * half].set(sin)
    reps = LANES // group
    return tuple(jnp.tile(a, (1, reps)) for a in (c, sa, sb))


def _identity_tables(n, width_blocks):
    ones = jnp.ones((n, LANES), F32)
    zeros = jnp.zeros((n, LANES), F32)
    return [ones, zeros, zeros][:width_blocks]


def _even_weights(w_in, w_out, w_uq, w_ukv):
    z = lambda r, c: jnp.zeros((r, c), F32)
    mla_in = MLA_Q_RANK + MLA_KV_RANK + MLA_ROPE
    kpe = w_in[:, MLA_Q_RANK + MLA_KV_RANK:mla_in]
    swa = w_in[:, mla_in:]
    swa_q = swa[:, :SWA_HEADS * SWA_DIM].reshape(D_MODEL, SWA_HEADS, SWA_DIM)
    pair_order = np.array([h for j in range(SWA_GROUP) for h in (j, SWA_GROUP + j)])
    swa_q = swa_q[:, pair_order, :].reshape(D_MODEL, SWA_HEADS * SWA_DIM)
    win = jnp.concatenate([
        w_in[:, :MLA_Q_RANK + MLA_KV_RANK], z(D_MODEL, 64), kpe, z(D_MODEL, 32), swa_q,
        swa[:, SWA_HEADS * SWA_DIM:]], axis=1)
    assert win.shape[1] == EVEN_COLS
    wuq = w_uq.reshape(MLA_Q_RANK, MLA_HEADS, MLA_NOPE + MLA_ROPE)
    wuq = jnp.pad(wuq, ((0, 0), (0, 0), (0, LANES - MLA_NOPE - MLA_ROPE))).reshape(MLA_Q_RANK, MLA_HEADS * LANES)
    wukv = w_ukv.reshape(MLA_KV_RANK, MLA_HEADS, MLA_NOPE + MLA_V)
    wuk = jnp.pad(wukv[:, :, :MLA_NOPE], ((0, 0), (0, 0), (0, LANES - MLA_NOPE))).reshape(MLA_KV_RANK, MLA_HEADS * LANES)
    wuv = wukv[:, :, MLA_NOPE:].reshape(MLA_KV_RANK, MLA_HEADS * MLA_V)
    n_a = MLA_HEADS * MLA_V
    wo_a = w_out[:n_a]
    wo_b = w_out[n_a:].reshape(SWA_HEADS, SWA_DIM, D_MODEL)[pair_order].reshape(SWA_HEADS * SWA_DIM, D_MODEL)
    return tuple(a.astype(BF16) for a in (win, wuq, wuk, wuv, wo_a, wo_b))


def _ffn_weights(w_up, conv_w, conv_b, w_down):
    def chunks(w):
        return w.reshape(D_MODEL, N_FF_CHUNKS, FF_CHUNK).transpose(1, 0, 2).astype(BF16)

    cw = jnp.concatenate([conv_w, conv_b[None], jnp.zeros((4, D_FF), F32)], axis=0)
    cw = cw.reshape(8, N_FF_CHUNKS, FF_CHUNK).transpose(1, 0, 2)
    wd = w_down.reshape(N_FF_CHUNKS, FF_CHUNK, D_MODEL).astype(BF16)
    return chunks(w_up[:, :D_FF]), chunks(w_up[:, D_FF:]), cw, wd


def kernel(x, c, ctx, c_ctx, ada_w, ada_b, pre_mix_g, post_mix_g, pre_ffn_g, post_ffn_g, ffn_up, ffn_conv_w,
           ffn_conv_b, ffn_down, even_w_in, even_w_out, mla_q_norm_g, mla_w_uq, mla_kv_norm_g, mla_w_ukv,
           swa_sink, odd_w_in, odd_w_out, na_rpb, gqa_q_g, gqa_k_g):
    batch, seq, _ = x.shape
    ctx_len = ctx.shape[1]
    depth = ada_w.shape[0]
    n_lat, n_ctx = batch * seq, batch * ctx_len

    pad_rows = -(batch + 1) % 8
    cc = jnp.concatenate([c, c_ctx[None], jnp.zeros((pad_rows, D_MODEL), F32)], axis=0)
    mods = _ada(cc, ada_w, ada_b).reshape(depth, cc.shape[0], 6, D_MODEL)
    mods = jnp.pad(mods, ((0, 0), (0, 0), (0, 2), (0, 0)))

    lat_rows = _Rows(n_lat, seq, 512, True)
    pre_rows = _Rows(n_lat, seq, 2 * PRE_HALF, True)
    ctx_rows = _Rows(n_ctx, ctx_len, ctx_len, False)

    cos_m, sin_m = _rope_angles(seq, MLA_ROPE)
    cos_s, sin_s = _rope_angles(seq, SWA_DIM)
    cos_g, sin_g = _rope_angles(seq, GQA_DIM)
    gc, gsa, gsb = _rope_lane_tables(cos_g, sin_g, 0, LANES)
    tab_even = jnp.concatenate(_rope_lane_tables(cos_m, sin_m, MLA_NOPE, LANES)
                               + _rope_lane_tables(cos_s, sin_s, 0, SWA_DIM), axis=1)
    tab_odd = jnp.concatenate([gc, gsa + gsb], axis=1)
    tab_even_ctx = jnp.concatenate(_identity_tables(ctx_len, 3) * 2, axis=1)
    tab_odd_ctx = jnp.concatenate(_identity_tables(ctx_len, 2), axis=1)

    x = x.reshape(n_lat, D_MODEL)
    ctx = ctx.reshape(n_ctx, D_MODEL)
    row = lambda v: v.reshape(1, -1)

    for layer in range(depth):
        need_ctx = layer < depth - 1
        i = layer // 2
        mod_lat = mods[layer, :batch]
        mod_ctx = mods[layer, batch:batch + 1]
        if layer % 2 == 0:
            win, wuq, wuk, wuv, wo_a, wo_b = _even_weights(even_w_in[i], even_w_out[i], mla_w_uq[i], mla_w_ukv[i])
            shared = (row(pre_mix_g[layer]), win, row(mla_q_norm_g[i]), wuq, row(mla_kv_norm_g[i]), wuk, wuv)
            qm, km, vm, qs, ks, vs = _pre_even(pre_rows, x, mod_lat, *shared, tab_even)
            qmc, kmc, vmc, qsc, ksc, vsc = _pre_even(ctx_rows, ctx, mod_ctx, *shared, tab_even_ctx)
            a_lat = _mla_attn(qm, kmc, vmc, km, vm, batch, seq, ctx_len, seq, 1024)
            b_lat = _swa_attn_lat(swa_sink[i], qs, ks, vs, ksc, vsc, batch, seq, ctx_len, 512)
            if need_ctx:
                a_ctx = _mla_attn(qmc, kmc, vmc, None, None, batch, ctx_len, ctx_len, seq, ctx_len)
                b_ctx = _swa_attn_ctx(swa_sink[i], qsc, ksc, vsc, batch, ctx_len)
        else:
            win = odd_w_in[i].astype(BF16)
            n_a = NA_HEADS * NA_DIM
            wo_a, wo_b = odd_w_out[i][:n_a].astype(BF16), odd_w_out[i][n_a:].astype(BF16)
            shared = (row(pre_mix_g[layer]), win, row(gqa_q_g[i]), row(gqa_k_g[i]))
            qn, kn, vn, qq, kq, vq = _pre_odd(pre_rows, x, mod_lat, *shared, tab_odd)
            qnc, knc, vnc, qqc, kqc, vqc = _pre_odd(ctx_rows, ctx, mod_ctx, *shared, tab_odd_ctx)
            na_tab = _na_bias_table(na_rpb[i], seq // GRID_W)
            a_lat = _na_attn(na_tab, qn, kn, vn, knc, vnc, batch, seq, ctx_len)
            b_lat = _gqa_attn(qq, kqc, vqc, kq, vq, batch, seq, ctx_len, 512)
            assert not need_ctx, "context-query attention for odd layers is only needed when depth > 2"
        wg, wv, cw, wd = _ffn_weights(ffn_up[layer], ffn_conv_w[layer], ffn_conv_b[layer], ffn_down[layer])
        g1, g2, g3 = row(post_mix_g[layer]), row(pre_ffn_g[layer]), row(post_ffn_g[layer])
        x = _mix_ffn(lat_rows, a_lat, b_lat, x, mod_lat, wo_a, wo_b, g1, g2, g3, wg, wv, cw, wd)
        if need_ctx:
            ctx = _mix_ffn(ctx_rows, a_ctx, b_ctx, ctx, mod_ctx, wo_a, wo_b, g1, g2, g3, wg, wv, cw, wd)
    return x.reshape(batch, seq, D_MODEL)
```

```python
import functools
import math

import jax
import jax.numpy as jnp
import numpy as np
from jax import lax
from jax.experimental import pallas as pl
from jax.experimental.pallas import tpu as pltpu

F32 = jnp.float32
BF16 = jnp.bfloat16

D_MODEL = 1024
GRID_W = 64
ROPE_BASE = 10000.0
EPS = 1e-6
NEG_INF = -1e30
LOG2E = math.log2(math.e)
KEY_CHUNK = 512
LANES = 128

MLA_HEADS, MLA_Q_RANK, MLA_KV_RANK, MLA_NOPE, MLA_ROPE, MLA_V = 8, 256, 128, 64, 32, 64
SWA_HEADS, SWA_KV_HEADS, SWA_DIM, SWA_WINDOW = 8, 2, 64, 128
NA_HEADS, NA_DIM, NA_KH, NA_KW = 8, 64, 8, 16
GQA_HEADS, GQA_KV_HEADS, GQA_DIM = 4, 2, 128
D_FF = 2816
FF_CHUNK = 256
N_FF_CHUNKS = D_FF // FF_CHUNK
HALO = 16

NA_TILE_ROWS = 4
NA_WIN_ROWS = NA_TILE_ROWS + NA_KH - 1

VMEM_LIMIT = 56 * 1024 * 1024


def _params():
    return pltpu.CompilerParams(vmem_limit_bytes=VMEM_LIMIT)


def _dot(a, b):
    return jnp.dot(a, b, preferred_element_type=F32)


def _dot_nt(a, b):
    return lax.dot_general(a, b, (((1,), (1,)), ((), ())), preferred_element_type=F32)


def _rms(x, g):
    return x * lax.rsqrt(jnp.mean(x * x, axis=-1, keepdims=True) + EPS) * g


def _rope(x, c, sa, sb, sha, shb):
    return x * c + pltpu.roll(x, sha, axis=1) * sa + pltpu.roll(x, shb, axis=1) * sb


def _lane_lo(shape):
    return lax.broadcasted_iota(jnp.int32, shape, len(shape) - 1) < (LANES // 2)


def _const_spec(shape):
    nd = len(shape)
    return pl.BlockSpec(shape, lambda *_: (0,) * nd, pipeline_mode=pl.Buffered(1))


def _ada_kernel(cc_ref, w_ref, b_ref, o_ref):
    cc = cc_ref[...]
    s = cc * (1.0 / (1.0 + jnp.exp(-cc)))
    o_ref[0] = _dot(s.astype(BF16), w_ref[0].astype(BF16)) + b_ref[0]


def _ada(cc, ada_w, ada_b):
    depth = ada_w.shape[0]
    rows = cc.shape[0]
    return pl.pallas_call(
        _ada_kernel,
        grid=(depth, 6),
        in_specs=[
            pl.BlockSpec((rows, D_MODEL), lambda l, j: (0, 0)),
            pl.BlockSpec((1, D_MODEL, D_MODEL), lambda l, j: (l, 0, j)),
            pl.BlockSpec((1, 1, D_MODEL), lambda l, j: (l, 0, j)),
        ],
        out_specs=pl.BlockSpec((1, rows, D_MODEL), lambda l, j: (l, 0, j)),
        out_shape=jax.ShapeDtypeStruct((depth, rows, 6 * D_MODEL), F32),
        compiler_params=_params(),
        name="ada",
    )(cc, ada_w, ada_b.reshape(depth, 1, 6 * D_MODEL))


class _Rows:
    def __init__(self, n, seq, tm, per_batch_mod):
        assert seq % tm == 0 and n % seq == 0
        self.n, self.seq, self.tm = n, seq, tm
        self.tps = seq // tm
        self.grid = (n // tm,)
        self.per_batch_mod = per_batch_mod

    def row_spec(self, width):
        return pl.BlockSpec((self.tm, width), lambda i: (i, 0))

    def mod_spec(self):
        if self.per_batch_mod:
            tps = self.tps
            return pl.BlockSpec((1, 8, D_MODEL), lambda i: (i // tps, 0, 0))
        return pl.BlockSpec((1, 8, D_MODEL), lambda i: (0, 0, 0))

    def table_spec(self, width):
        tps = self.tps
        return pl.BlockSpec((self.tm, width), lambda i: (i % tps, 0))


PRE_HALF = 512


def _row_halves(tm):
    half = min(tm, PRE_HALF)
    assert tm % half == 0
    return [slice(r, r + half) for r in range(0, tm, half)]


def _modulated(x_ref, mod_ref, g_ref, rows, shift_row, scale_row):
    x = x_ref[rows, :]
    shift = mod_ref[0, shift_row:shift_row + 1, :]
    scale = mod_ref[0, scale_row:scale_row + 1, :]
    return _rms(x, g_ref[...]) * (1.0 + scale) + shift


EVEN_COLS = 1280


def _pre_even_kernel(x_ref, mod_ref, g_ref, win_ref, qg_ref, wuq_ref, kvg_ref, wuk_ref, wuv_ref,
                     tab_ref, qm_ref, km_ref, vm_ref, qs_ref, ks_ref, vs_ref):
    mla_scale = LOG2E * (MLA_NOPE + MLA_ROPE) ** -0.5
    swa_scale = LOG2E * SWA_DIM ** -0.5
    for rows in _row_halves(x_ref.shape[0]):
        h = _modulated(x_ref, mod_ref, g_ref, rows, 0, 1).astype(BF16)
        mc, ma, mb = tab_ref[rows, 0:128], tab_ref[rows, 128:256], tab_ref[rows, 256:384]
        sc, sa, sb = tab_ref[rows, 384:512], tab_ref[rows, 512:640], tab_ref[rows, 640:768]

        pa = _dot(h, win_ref[:, 0:512])
        cq = _rms(pa[:, 0:256], qg_ref[...]).astype(BF16)
        ckv = _rms(pa[:, 256:384], kvg_ref[...]).astype(BF16)
        kpe = _rope(pa[:, 384:512], mc, ma, mb, 112, 16)
        q = _dot(cq, wuq_ref[...])
        kn = _dot(ckv, wuk_ref[...])
        for j in range(MLA_HEADS):
            sl = slice(j * LANES, (j + 1) * LANES)
            qm_ref[rows, sl] = (_rope(q[:, sl], mc, ma, mb, 112, 16) * mla_scale).astype(BF16)
            km_ref[rows, sl] = (kn[:, sl] + kpe).astype(BF16)
        vm_ref[rows, :] = _dot(ckv, wuv_ref[...]).astype(BF16)

        pq = _dot(h, win_ref[:, 512:1024])
        for j in range(4):
            sl = slice(j * LANES, (j + 1) * LANES)
            qs_ref[rows, sl] = (_rope(pq[:, sl], sc, sa, sb, 96, 32) * swa_scale).astype(BF16)
        pkv = _dot(h, win_ref[:, 1024:1280])
        ks_ref[rows, :] = _rope(pkv[:, 0:128], sc, sa, sb, 96, 32).astype(BF16)
        vs_ref[rows, :] = pkv[:, 128:256].astype(BF16)


def _pre_even(rows, x, mod, g, win, qg, wuq, kvg, wuk, wuv, tab):
    n = rows.n
    widths = (1024, 1024, 512, 512, 128, 128)
    return pl.pallas_call(
        _pre_even_kernel,
        grid=rows.grid,
        in_specs=[
            rows.row_spec(D_MODEL), rows.mod_spec(), _const_spec((1, D_MODEL)),
            _const_spec(win.shape), _const_spec(qg.shape), _const_spec(wuq.shape),
            _const_spec(kvg.shape), _const_spec(wuk.shape), _const_spec(wuv.shape),
            rows.table_spec(768),
        ],
        out_specs=[rows.row_spec(w) for w in widths],
        out_shape=[jax.ShapeDtypeStruct((n, w), BF16) for w in widths],
        compiler_params=_params(),
        name="pre_even",
    )(x, mod, g, win, qg, wuq, kvg, wuk, wuv, tab)


def _pre_odd_kernel(x_ref, mod_ref, g_ref, win_ref, qg_ref, kg_ref, tab_ref,
                    qn_ref, kn_ref, vn_ref, qq_ref, kq_ref, vq_ref):
    na_scale = LOG2E * NA_DIM ** -0.5
    gqa_scale = LOG2E * GQA_DIM ** -0.5
    for rows in _row_halves(x_ref.shape[0]):
        h = _modulated(x_ref, mod_ref, g_ref, rows, 0, 1).astype(BF16)
        gc, gs = tab_ref[rows, 0:128], tab_ref[rows, 128:256]

        def normed(blk, gain, gc=gc, gs=gs):
            y = _rms(blk, gain)
            return y * gc + pltpu.roll(y, 64, axis=1) * gs

        qn_ref[rows, :] = (_dot(h, win_ref[:, 0:512]) * na_scale).astype(BF16)
        kn_ref[rows, :] = _dot(h, win_ref[:, 512:1024]).astype(BF16)
        vn_ref[rows, :] = _dot(h, win_ref[:, 1024:1536]).astype(BF16)
        pq = _dot(h, win_ref[:, 1536:2048])
        for j in range(GQA_HEADS):
            sl = slice(j * LANES, (j + 1) * LANES)
            qq_ref[rows, sl] = (normed(pq[:, sl], qg_ref[...]) * gqa_scale).astype(BF16)
        pkv = _dot(h, win_ref[:, 2048:2560])
        for j in range(GQA_KV_HEADS):
            sl = slice(j * LANES, (j + 1) * LANES)
            kq_ref[rows, sl] = normed(pkv[:, sl], kg_ref[...]).astype(BF16)
        vq_ref[rows, :] = pkv[:, 256:512].astype(BF16)


def _pre_odd(rows, x, mod, g, win, qg, kg, tab):
    n = rows.n
    widths = (512, 512, 512, 512, 256, 256)
    return pl.pallas_call(
        _pre_odd_kernel,
        grid=rows.grid,
        in_specs=[
            rows.row_spec(D_MODEL), rows.mod_spec(), _const_spec((1, D_MODEL)),
            _const_spec(win.shape), _const_spec(qg.shape), _const_spec(kg.shape),
            rows.table_spec(256),
        ],
        out_specs=[rows.row_spec(w) for w in widths],
        out_shape=[jax.ShapeDtypeStruct((n, w), BF16) for w in widths],
        compiler_params=_params(),
        name="pre_odd",
    )(x, mod, g, win, qg, kg, tab)


def _attend(q, chunks, sink=None):
    rows = q.shape[0]
    if sink is None:
        m, acc = None, None
    else:
        m = sink
        acc = jnp.concatenate([jnp.zeros((rows, LANES), F32), jnp.ones((rows, LANES), F32)], axis=1)
    for k, v, bias in chunks:
        s = _dot_nt(q, k)
        if bias is not None:
            s = s + bias
        cm = jnp.max(s, axis=-1, keepdims=True)
        m_new = cm if m is None else jnp.maximum(m, cm)
        p = jnp.exp2(s - m_new).astype(BF16)
        vx = jnp.concatenate([v, jnp.ones(v.shape, BF16)], axis=1)
        pv = _dot(p, vx)
        acc = pv if acc is None else jnp.exp2(m - m_new) * acc + pv
        m = m_new
    return acc[:, :LANES] / acc[:, LANES:]


def _chunks(k_ref, v_ref, lanes, size):
    n = k_ref.shape[0]
    assert n % size == 0
    return [(k_ref[c * size:(c + 1) * size, lanes], v_ref[c * size:(c + 1) * size, :], None)
            for c in range(n // size)]


def _mla_kernel(has_lat, q_ref, kc_ref, vc_ref, *rest):
    if has_lat:
        kl_ref, vl_ref, o_ref = rest
    else:
        (o_ref,) = rest
    outs = []
    for e in range(2):
        sl = slice(e * LANES, (e + 1) * LANES)
        chunks = _chunks(kc_ref, vc_ref, sl, kc_ref.shape[0])
        if has_lat:
            chunks += _chunks(kl_ref, vl_ref, sl, KEY_CHUNK)
        outs.append(_attend(q_ref[:, sl], chunks))
    o_ref[...] = jnp.where(_lane_lo(outs[0].shape), outs[0], outs[1]).astype(BF16)


def _mla_attn(q, kc, vc, kl, vl, batch, q_seq, ctx_len, lat_seq, tq):
    has_lat = kl is not None
    nq = q_seq // tq
    in_specs = [
        pl.BlockSpec((tq, 256), lambda b, j, t: (b * nq + t, j)),
        pl.BlockSpec((ctx_len, 256), lambda b, j, t: (b, j)),
        pl.BlockSpec((ctx_len, 128), lambda b, j, t: (b, j)),
    ]
    args = [q, kc, vc]
    if has_lat:
        in_specs += [
            pl.BlockSpec((lat_seq, 256), lambda b, j, t: (b, j)),
            pl.BlockSpec((lat_seq, 128), lambda b, j, t: (b, j)),
        ]
        args += [kl, vl]
    return pl.pallas_call(
        functools.partial(_mla_kernel, has_lat),
        grid=(batch, MLA_HEADS // 2, nq),
        in_specs=in_specs,
        out_specs=pl.BlockSpec((tq, 128), lambda b, j, t: (b * nq + t, j)),
        out_shape=jax.ShapeDtypeStruct((batch * q_seq, MLA_HEADS * MLA_V), BF16),
        compiler_params=_params(),
        name="mla_attn_lat" if has_lat else "mla_attn_ctx",
    )(*args)


def _gqa_kernel(q_ref, kc_ref, vc_ref, kl_ref, vl_ref, o_ref):
    tq = q_ref.shape[0]
    per_group = GQA_HEADS // GQA_KV_HEADS
    q = jnp.concatenate([q_ref[:, e * LANES:(e + 1) * LANES] for e in range(per_group)], axis=0)
    every = slice(None)
    chunks = _chunks(kc_ref, vc_ref, every, kc_ref.shape[0]) + _chunks(kl_ref, vl_ref, every, KEY_CHUNK)
    o = _attend(q, chunks).astype(BF16)
    for e in range(per_group):
        o_ref[:, e * LANES:(e + 1) * LANES] = o[e * tq:(e + 1) * tq]


def _gqa_attn(q, kc, vc, kl, vl, batch, seq, ctx_len, tq):
    nq = seq // tq
    return pl.pallas_call(
        _gqa_kernel,
        grid=(batch, GQA_KV_HEADS, nq),
        in_specs=[
            pl.BlockSpec((tq, 256), lambda b, g, t: (b * nq + t, g)),
            pl.BlockSpec((ctx_len, 128), lambda b, g, t: (b, g)),
            pl.BlockSpec((ctx_len, 128), lambda b, g, t: (b, g)),
            pl.BlockSpec((seq, 128), lambda b, g, t: (b, g)),
            pl.BlockSpec((seq, 128), lambda b, g, t: (b, g)),
        ],
        out_specs=pl.BlockSpec((tq, 256), lambda b, g, t: (b * nq + t, g)),
        out_shape=jax.ShapeDtypeStruct((batch * seq, GQA_HEADS * GQA_DIM), BF16),
        compiler_params=_params(),
        name="gqa_attn",
    )(q, kc, vc, kl, vl)


SWA_QB = 128
SWA_GROUP = SWA_HEADS // SWA_KV_HEADS


def _swa_group_lhs(q_ref, r0, rows, g):
    lo = _lane_lo((rows, LANES))
    keep = lo if g == 0 else jnp.logical_not(lo)
    blocks = [jnp.where(keep, q_ref[r0:r0 + rows, j * LANES:(j + 1) * LANES], jnp.zeros((), BF16))
              for j in range(SWA_GROUP)]
    return jnp.concatenate(blocks, axis=0)


def _swa_sink_col(sink_ref, g, rows):
    blk = lax.broadcasted_iota(jnp.int32, (SWA_GROUP * rows, 1), 0) // rows
    col = jnp.full((SWA_GROUP * rows, 1), sink_ref[g * SWA_GROUP], F32)
    for j in range(1, SWA_GROUP):
        col = jnp.where(blk == j, sink_ref[g * SWA_GROUP + j], col)
    return col * LOG2E


def _swa_store(o_ref, r0, rows, o_groups):
    lo = _lane_lo((rows, LANES))
    for j in range(SWA_GROUP):
        a = o_groups[0][j * rows:(j + 1) * rows]
        b = o_groups[1][j * rows:(j + 1) * rows]
        o_ref[r0:r0 + rows, j * LANES:(j + 1) * LANES] = jnp.where(lo, a, b).astype(BF16)


SWA_BAND = SWA_QB + 2 * SWA_WINDOW


def _swa_mask_table():
    ii = np.arange(SWA_GROUP * SWA_QB)[:, None] % SWA_QB
    jj = np.arange(SWA_BAND)[None, :]
    in_window = np.abs(jj - SWA_WINDOW - ii) <= SWA_WINDOW
    first = in_window & (jj >= SWA_WINDOW)
    last = in_window & (jj < SWA_WINDOW + SWA_QB)
    return jnp.asarray(np.where(np.stack([first, in_window, last]), 0.0, NEG_INF), F32)


def _swa_lat_kernel(tq, sink_ref, mask_ref, q_ref, kp_ref, kt_ref, kn_ref, vp_ref, vt_ref, vn_ref,
                    kc_ref, vc_ref, o_ref):
    t = pl.program_id(1)
    kband = jnp.concatenate([kp_ref[...], kt_ref[...], kn_ref[...]], axis=0)
    vband = jnp.concatenate([vp_ref[...], vt_ref[...], vn_ref[...]], axis=0)
    band = SWA_BAND
    n_blocks = tq // SWA_QB
    for i in range(n_blocks):
        variant = 1
        if i == 0:
            variant = jnp.where(t == 0, 0, variant)
        if i == n_blocks - 1:
            variant = jnp.where(t == pl.num_programs(1) - 1, 2, variant)
        bias = mask_ref[variant]
        kb = kband[i * SWA_QB:i * SWA_QB + band]
        vb = vband[i * SWA_QB:i * SWA_QB + band]
        o_groups = []
        for g in range(SWA_KV_HEADS):
            lhs = _swa_group_lhs(q_ref, i * SWA_QB, SWA_QB, g)
            o_groups.append(_attend(lhs, [(kc_ref[...], vc_ref[...], None), (kb, vb, bias)],
                                    _swa_sink_col(sink_ref, g, SWA_QB)))
        _swa_store(o_ref, i * SWA_QB, SWA_QB, o_groups)


def _swa_ctx_kernel(ctx_len, sink_ref, q_ref, kc_ref, vc_ref, o_ref):
    o_groups = []
    for g in range(SWA_KV_HEADS):
        lhs = _swa_group_lhs(q_ref, 0, ctx_len, g)
        o_groups.append(_attend(lhs, [(kc_ref[...], vc_ref[...], None)], _swa_sink_col(sink_ref, g, ctx_len)))
    _swa_store(o_ref, 0, ctx_len, o_groups)


def _swa_attn_lat(sink, q, k, v, kc, vc, batch, seq, ctx_len, tq):
    nq = seq // tq
    nb = seq // SWA_WINDOW
    per = tq // SWA_WINDOW
    prev_spec = pl.BlockSpec((SWA_WINDOW, 128), lambda b, t: (b * nb + jnp.maximum(t * per - 1, 0), 0))
    next_spec = pl.BlockSpec((SWA_WINDOW, 128), lambda b, t: (b * nb + jnp.minimum(t * per + per, nb - 1), 0))
    tile_spec = pl.BlockSpec((tq, 128), lambda b, t: (b * nq + t, 0))
    ctx_spec = pl.BlockSpec((ctx_len, 128), lambda b, t: (b, 0))
    mask = _swa_mask_table()
    return pl.pallas_call(
        functools.partial(_swa_lat_kernel, tq),
        grid=(batch, nq),
        in_specs=[
            pl.BlockSpec(memory_space=pltpu.SMEM),
            pl.BlockSpec(mask.shape, lambda b, t: (0, 0, 0)),
            pl.BlockSpec((tq, 512), lambda b, t: (b * nq + t, 0)),
            prev_spec, tile_spec, next_spec, prev_spec, tile_spec, next_spec,
            ctx_spec, ctx_spec,
        ],
        out_specs=pl.BlockSpec((tq, 512), lambda b, t: (b * nq + t, 0)),
        out_shape=jax.ShapeDtypeStruct((batch * seq, SWA_HEADS * SWA_DIM), BF16),
        compiler_params=_params(),
        name="swa_attn_lat",
    )(sink, mask, q, k, k, k, v, v, v, kc, vc)


def _swa_attn_ctx(sink, q, kc, vc, batch, ctx_len):
    return pl.pallas_call(
        functools.partial(_swa_ctx_kernel, ctx_len),
        grid=(batch,),
        in_specs=[
            pl.BlockSpec(memory_space=pltpu.SMEM),
            pl.BlockSpec((ctx_len, 512), lambda b: (b, 0)),
            pl.BlockSpec((ctx_len, 128), lambda b: (b, 0)),
            pl.BlockSpec((ctx_len, 128), lambda b: (b, 0)),
        ],
        out_specs=pl.BlockSpec((ctx_len, 512), lambda b: (b, 0)),
        out_shape=jax.ShapeDtypeStruct((batch * ctx_len, SWA_HEADS * SWA_DIM), BF16),
        compiler_params=_params(),
        name="swa_attn_ctx",
    )(sink, q, kc, vc)


NA_TILES_PER_STEP = 8


def _na_kernel(n_tiles, tab_ref, q_ref, k_ref, v_ref, kc_ref, vc_ref, o_ref):
    grid_rows = n_tiles * NA_TILE_ROWS
    tq = NA_TILE_ROWS * GRID_W
    lo = _lane_lo((tq, LANES))
    for u in range(NA_TILES_PER_STEP):
        t = pl.program_id(2) * NA_TILES_PER_STEP + u
        ws = jnp.clip(t * NA_TILE_ROWS - NA_KH // 2, 0, grid_rows - NA_WIN_ROWS)
        variant = jnp.where(t == 0, 0, jnp.where(t == n_tiles - 1, 2, 1))
        start = pl.multiple_of(ws * GRID_W, GRID_W)
        kwin = k_ref[pl.ds(start, NA_WIN_ROWS * GRID_W), :]
        vwin = v_ref[pl.ds(start, NA_WIN_ROWS * GRID_W), :]
        q = q_ref[u * tq:(u + 1) * tq, :]
        zero = jnp.zeros((), BF16)
        lhs = jnp.concatenate([jnp.where(lo, q, zero), jnp.where(lo, zero, q)], axis=0)
        bias = tab_ref[variant].reshape(2 * tq, NA_WIN_ROWS * GRID_W)
        o = _attend(lhs, [(kc_ref[...], vc_ref[...], None), (kwin, vwin, bias)])
        o_ref[u * tq:(u + 1) * tq, :] = jnp.where(lo, o[:tq], o[tq:]).astype(BF16)


def _na_attn(tab, q, k, v, kc, vc, batch, seq, ctx_len):
    tq = NA_TILE_ROWS * GRID_W
    n_tiles = seq // tq
    nk = NA_WIN_ROWS * GRID_W
    assert n_tiles % NA_TILES_PER_STEP == 0
    n_steps = n_tiles // NA_TILES_PER_STEP
    tq_step = tq * NA_TILES_PER_STEP
    return pl.pallas_call(
        functools.partial(_na_kernel, n_tiles),
        grid=(NA_HEADS // 2, batch, n_steps),
        in_specs=[
            pl.BlockSpec((3, 2, tq, nk), lambda j, b, t: (0, j, 0, 0)),
            pl.BlockSpec((tq_step, 128), lambda j, b, t: (b * n_steps + t, j)),
            pl.BlockSpec((seq, 128), lambda j, b, t: (b, j)),
            pl.BlockSpec((seq, 128), lambda j, b, t: (b, j)),
            pl.BlockSpec((ctx_len, 128), lambda j, b, t: (b, j)),
            pl.BlockSpec((ctx_len, 128), lambda j, b, t: (b, j)),
        ],
        out_specs=pl.BlockSpec((tq_step, 128), lambda j, b, t: (b * n_steps + t, j)),
        out_shape=jax.ShapeDtypeStruct((batch * seq, NA_HEADS * NA_DIM), BF16),
        compiler_params=_params(),
        name="na_attn",
    )(tab, q, k, v, kc, vc)


def _na_bias_table(rpb, grid_rows):
    cols = np.arange(GRID_W)
    col_start = np.clip(cols - NA_KW // 2, 0, GRID_W - NA_KW)
    col_ok = (cols[None, :] >= col_start[:, None]) & (cols[None, :] < col_start[:, None] + NA_KW)
    dc = np.clip(cols[None, :] - cols[:, None], -(NA_KW - 1), NA_KW - 1) + NA_KW - 1
    n_heads, n_dr = rpb.shape[0], rpb.shape[1]
    by_col = jnp.take(rpb.astype(F32) * LOG2E, jnp.asarray(dc.reshape(-1)), axis=2).reshape(n_heads, n_dr, GRID_W, GRID_W)
    by_col = jnp.where(jnp.asarray(col_ok)[None, None], by_col, NEG_INF)
    by_col = jnp.concatenate([by_col, jnp.full((n_heads, 1, GRID_W, GRID_W), NEG_INF, F32)], axis=1)
    n_tiles = grid_rows // NA_TILE_ROWS
    variants = []
    for t in (0, min(2, n_tiles - 1), n_tiles - 1):
        r0 = t * NA_TILE_ROWS
        ws = int(np.clip(r0 - NA_KH // 2, 0, grid_rows - NA_WIN_ROWS))
        r = r0 + np.arange(NA_TILE_ROWS)
        rs = np.clip(r - NA_KH // 2, 0, grid_rows - NA_KH)
        key_row = ws + np.arange(NA_WIN_ROWS)
        row_ok = (key_row[None, :] >= rs[:, None]) & (key_row[None, :] < rs[:, None] + NA_KH)
        dr = np.where(row_ok, key_row[None, :] - r[:, None] + NA_KH - 1, n_dr)
        rows = []
        for i in range(NA_TILE_ROWS):
            rows.append(jnp.concatenate([by_col[:, int(d)] for d in dr[i]], axis=-1))
        variants.append(jnp.concatenate(rows, axis=1))
    return jnp.stack(variants, axis=0)


def _gelu_tanh(x):
    return x * (0.5 * (1.0 + jnp.tanh(math.sqrt(2.0 / math.pi) * (x + 0.044715 * (x * x * x)))))


def _mix_ffn_kernel(tm, tps, a_ref, b_ref, x_ref, modp_ref, modf_ref, wa_ref, wb_ref, g1_ref, g2_ref, g3_ref,
                    wg_ref, wv_ref, cw_ref, wd_ref, o_ref, hh_ref, x1_ref, ug_ref, uv_ref, act_ref, acc_ref):
    i = pl.program_id(0)
    zero = jnp.zeros((), BF16)

    @pl.when(i == 0)
    def _():
        hh_ref[...] = jnp.zeros_like(hh_ref)
        x1_ref[...] = jnp.zeros_like(x1_ref)

    def step(s):
        o = 1 - s
        y = _dot(a_ref[...], wa_ref[...]) + _dot(b_ref[...], wb_ref[...])
        x1 = x_ref[...] + modp_ref[0, 2:3, :] * _rms(y, g1_ref[...])
        h = (_rms(x1, g2_ref[...]) * (1.0 + modp_ref[0, 4:5, :]) + modp_ref[0, 3:4, :]).astype(BF16)
        x1_ref[s] = x1
        hh_ref[s, HALO:HALO + tm] = h
        first = i % tps == 0
        hh_ref[s, 0:HALO] = jnp.where(first, zero, hh_ref[o, tm:tm + HALO])
        hh_ref[o, HALO + tm:] = jnp.where(first, zero, h[0:HALO])

        def up(c, slot):
            uv_ref[slot] = _dot(hh_ref[o, HALO:HALO + tm], wv_ref[c])
            ug_ref[slot] = _dot(hh_ref[o], wg_ref[c])

        def activate(c, slot):
            cw = cw_ref[c]
            gate = cw[3:4]
            for tap in range(3):
                gate = gate + ug_ref[slot, HALO - 1 + tap:HALO - 1 + tap + tm] * cw[tap:tap + 1]
            act_ref[c] = (_gelu_tanh(gate) * uv_ref[slot]).astype(BF16)

        up(0, 0)
        for c in range(N_FF_CHUNKS):
            if c + 1 < N_FF_CHUNKS:
                up(c + 1, (c + 1) % 2)
            activate(c, c % 2)

        for nb in range(D_MODEL // FF_CHUNK):
            cols = slice(nb * FF_CHUNK, (nb + 1) * FF_CHUNK)
            f = _dot(act_ref[0], wd_ref[0, :, cols])
            for c in range(1, N_FF_CHUNKS):
                f = f + _dot(act_ref[c], wd_ref[c, :, cols])
            acc_ref[:, cols] = f
        o_ref[...] = x1_ref[o] + modf_ref[0, 5:6, :] * _rms(acc_ref[...], g3_ref[...])

    pl.when(i % 2 == 0)(lambda: step(0))
    pl.when(i % 2 == 1)(lambda: step(1))


def _mix_ffn(rows, a, b, x, mod, wa, wb, g1, g2, g3, wg, wv, cw, wd):
    n, tm, tps = rows.n, rows.tm, rows.tps
    n_tiles = n // tm
    assert n_tiles % tps == 0

    def tile_in(width):
        return pl.BlockSpec((tm, width), lambda i: (jnp.minimum(i, n_tiles - 1), 0))

    def mod_spec(tile_of_step):
        if rows.per_batch_mod:
            return pl.BlockSpec((1, 8, D_MODEL), lambda i: (tile_of_step(i) // tps, 0, 0))
        return pl.BlockSpec((1, 8, D_MODEL), lambda i: (0, 0, 0))

    row = _const_spec((1, D_MODEL))
    return pl.pallas_call(
        functools.partial(_mix_ffn_kernel, tm, tps),
        grid=(n_tiles + 1,),
        in_specs=[
            tile_in(a.shape[1]), tile_in(b.shape[1]), tile_in(D_MODEL),
            mod_spec(lambda i: jnp.minimum(i, n_tiles - 1)), mod_spec(lambda i: jnp.maximum(i - 1, 0)),
            _const_spec(wa.shape), _const_spec(wb.shape), row, row, row,
            _const_spec(wg.shape), _const_spec(wv.shape), _const_spec(cw.shape), _const_spec(wd.shape),
        ],
        out_specs=pl.BlockSpec((tm, D_MODEL), lambda i: (jnp.maximum(i - 1, 0), 0)),
        out_shape=jax.ShapeDtypeStruct((n, D_MODEL), F32),
        scratch_shapes=[
            pltpu.VMEM((2, tm + 2 * HALO, D_MODEL), BF16),
            pltpu.VMEM((2, tm, D_MODEL), F32),
            pltpu.VMEM((2, tm + 2 * HALO, FF_CHUNK), F32),
            pltpu.VMEM((2, tm, FF_CHUNK), F32),
            pltpu.VMEM((N_FF_CHUNKS, tm, FF_CHUNK), BF16),
            pltpu.VMEM((tm, D_MODEL), F32),
        ],
        compiler_params=_params(),
        name="mix_ffn",
    )(a, b, x, mod, mod, wa, wb, g1, g2, g3, wg, wv, cw, wd)


def _rope_angles(n, d):
    t = jnp.arange(n)
    rows = (t // GRID_W).astype(F32)
    cols = (t % GRID_W).astype(F32)
    freqs = ROPE_BASE ** (-jnp.arange(d // 4, dtype=F32) * 4.0 / d)
    ang = jnp.concatenate([rows[:, None] * freqs, cols[:, None] * freqs], axis=-1)
    return jnp.cos(ang), jnp.sin(ang)


def _rope_lane_tables(cos, sin, offset, group):
    n, half = cos.shape
    c = jnp.ones((n, group), F32)
    sa = jnp.zeros((n, group), F32)
    sb = jnp.zeros((n, group), F32)
    c = c.at[:, offset:offset + half].set(cos).at[:, offset + half:offset + 2 * half].set(cos)
    sa = sa.at[:, offset:offset + half].set(-sin)
    sb = sb.at[:, offset + half:offset + 2 * half].set(sin)
    reps = LANES // group
    return tuple(jnp.tile(a, (1, reps)) for a in (c, sa, sb))


def _identity_tables(n, width_blocks):
    ones = jnp.ones((n, LANES), F32)
    zeros = jnp.zeros((n, LANES), F32)
    return [ones, zeros, zeros][:width_blocks]


def _even_weights(w_in, w_out, w_uq, w_ukv):
    z = lambda r, c: jnp.zeros((r, c), F32)
    mla_in = MLA_Q_RANK + MLA_KV_RANK + MLA_ROPE
    kpe = w_in[:, MLA_Q_RANK + MLA_KV_RANK:mla_in]
    swa = w_in[:, mla_in:]
    swa_q = swa[:, :SWA_HEADS * SWA_DIM].reshape(D_MODEL, SWA_HEADS, SWA_DIM)
    pair_order = np.array([h for j in range(SWA_GROUP) for h in (j, SWA_GROUP + j)])
    swa_q = swa_q[:, pair_order, :].reshape(D_MODEL, SWA_HEADS * SWA_DIM)
    win = jnp.concatenate([
        w_in[:, :MLA_Q_RANK + MLA_KV_RANK], z(D_MODEL, 64), kpe, z(D_MODEL, 32), swa_q,
        swa[:, SWA_HEADS * SWA_DIM:]], axis=1)
    assert win.shape[1] == EVEN_COLS
    wuq = w_uq.reshape(MLA_Q_RANK, MLA_HEADS, MLA_NOPE + MLA_ROPE)
    wuq = jnp.pad(wuq, ((0, 0), (0, 0), (0, LANES - MLA_NOPE - MLA_ROPE))).reshape(MLA_Q_RANK, MLA_HEADS * LANES)
    wukv = w_ukv.reshape(MLA_KV_RANK, MLA_HEADS, MLA_NOPE + MLA_V)
    wuk = jnp.pad(wukv[:, :, :MLA_NOPE], ((0, 0), (0, 0), (0, LANES - MLA_NOPE))).reshape(MLA_KV_RANK, MLA_HEADS * LANES)
    wuv = wukv[:, :, MLA_NOPE:].reshape(MLA_KV_RANK, MLA_HEADS * MLA_V)
    n_a = MLA_HEADS * MLA_V
    wo_a = w_out[:n_a]
    wo_b = w_out[n_a:].reshape(SWA_HEADS, SWA_DIM, D_MODEL)[pair_order].reshape(SWA_HEADS * SWA_DIM, D_MODEL)
    return tuple(a.astype(BF16) for a in (win, wuq, wuk, wuv, wo_a, wo_b))


def _ffn_weights(w_up, conv_w, conv_b, w_down):
    def chunks(w):
        return w.reshape(D_MODEL, N_FF_CHUNKS, FF_CHUNK).transpose(1, 0, 2).astype(BF16)

    cw = jnp.concatenate([conv_w, conv_b[None], jnp.zeros((4, D_FF), F32)], axis=0)
    cw = cw.reshape(8, N_FF_CHUNKS, FF_CHUNK).transpose(1, 0, 2)
    wd = w_down.reshape(N_FF_CHUNKS, FF_CHUNK, D_MODEL).astype(BF16)
    return chunks(w_up[:, :D_FF]), chunks(w_up[:, D_FF:]), cw, wd


def kernel(x, c, ctx, c_ctx, ada_w, ada_b, pre_mix_g, post_mix_g, pre_ffn_g, post_ffn_g, ffn_up, ffn_conv_w,
           ffn_conv_b, ffn_down, even_w_in, even_w_out, mla_q_norm_g, mla_w_uq, mla_kv_norm_g, mla_w_ukv,
           swa_sink, odd_w_in, odd_w_out, na_rpb, gqa_q_g, gqa_k_g):
    batch, seq, _ = x.shape
    ctx_len = ctx.shape[1]
    depth = ada_w.shape[0]
    n_lat, n_ctx = batch * seq, batch * ctx_len

    pad_rows = -(batch + 1) % 8
    cc = jnp.concatenate([c, c_ctx[None], jnp.zeros((pad_rows, D_MODEL), F32)], axis=0)
    mods = _ada(cc, ada_w, ada_b).reshape(depth, cc.shape[0], 6, D_MODEL)
    mods = jnp.pad(mods, ((0, 0), (0, 0), (0, 2), (0, 0)))

    lat_rows = _Rows(n_lat, seq, 512, True)
    pre_rows = _Rows(n_lat, seq, 2 * PRE_HALF, True)
    ctx_rows = _Rows(n_ctx, ctx_len, ctx_len, False)

    cos_m, sin_m = _rope_angles(seq, MLA_ROPE)
    cos_s, sin_s = _rope_angles(seq, SWA_DIM)
    cos_g, sin_g = _rope_angles(seq, GQA_DIM)
    gc, gsa, gsb = _rope_lane_tables(cos_g, sin_g, 0, LANES)
    tab_even = jnp.concatenate(_rope_lane_tables(cos_m, sin_m, MLA_NOPE, LANES)
                               + _rope_lane_tables(cos_s, sin_s, 0, SWA_DIM), axis=1)
    tab_odd = jnp.concatenate([gc, gsa + gsb], axis=1)
    tab_even_ctx = jnp.concatenate(_identity_tables(ctx_len, 3) * 2, axis=1)
    tab_odd_ctx = jnp.concatenate(_identity_tables(ctx_len, 2), axis=1)

    x = x.reshape(n_lat, D_MODEL)
    ctx = ctx.reshape(n_ctx, D_MODEL)
    row = lambda v: v.reshape(1, -1)

    for layer in range(depth):
        need_ctx = layer < depth - 1
        i = layer // 2
        mod_lat = mods[layer, :batch]
        mod_ctx = mods[layer, batch:batch + 1]
        if layer % 2 == 0:
            win, wuq, wuk, wuv, wo_a, wo_b = _even_weights(even_w_in[i], even_w_out[i], mla_w_uq[i], mla_w_ukv[i])
            shared = (row(pre_mix_g[layer]), win, row(mla_q_norm_g[i]), wuq, row(mla_kv_norm_g[i]), wuk, wuv)
            qm, km, vm, qs, ks, vs = _pre_even(pre_rows, x, mod_lat, *shared, tab_even)
            qmc, kmc, vmc, qsc, ksc, vsc = _pre_even(ctx_rows, ctx, mod_ctx, *shared, tab_even_ctx)
            a_lat = _mla_attn(qm, kmc, vmc, km, vm, batch, seq, ctx_len, seq, 1024)
            b_lat = _swa_attn_lat(swa_sink[i], qs, ks, vs, ksc, vsc, batch, seq, ctx_len, 512)
            if need_ctx:
                a_ctx = _mla_attn(qmc, kmc, vmc, None, None, batch, ctx_len, ctx_len, seq, ctx_len)
                b_ctx = _swa_attn_ctx(swa_sink[i], qsc, ksc, vsc, batch, ctx_len)
        else:
            win = odd_w_in[i].astype(BF16)
            n_a = NA_HEADS * NA_DIM
            wo_a, wo_b = odd_w_out[i][:n_a].astype(BF16), odd_w_out[i][n_a:].astype(BF16)
            shared = (row(pre_mix_g[layer]), win, row(gqa_q_g[i]), row(gqa_k_g[i]))
            qn, kn, vn, qq, kq, vq = _pre_odd(pre_rows, x, mod_lat, *shared, tab_odd)
            qnc, knc, vnc, qqc, kqc, vqc = _pre_odd(ctx_rows, ctx, mod_ctx, *shared, tab_odd_ctx)
            na_tab = _na_bias_table(na_rpb[i], seq // GRID_W)
            a_lat = _na_attn(na_tab, qn, kn, vn, knc, vnc, batch, seq, ctx_len)
            b_lat = _gqa_attn(qq, kqc, vqc, kq, vq, batch, seq, ctx_len, 512)
            assert not need_ctx, "context-query attention for odd layers is only needed when depth > 2"
        wg, wv, cw, wd = _ffn_weights(ffn_up[layer], ffn_conv_w[layer], ffn_conv_b[layer], ffn_down[layer])
        g1, g2, g3 = row(post_mix_g[layer]), row(pre_ffn_g[layer]), row(post_ffn_g[layer])
        x = _mix_ffn(lat_rows, a_lat, b_lat, x, mod_lat, wo_a, wo_b, g1, g2, g3, wg, wv, cw, wd)
        if need_ctx:
            ctx = _mix_ffn(ctx_rows, a_ctx, b_ctx, ctx, mod_ctx, wo_a, wo_b, g1, g2, g3, wg, wv, cw, wd)
    return x.reshape(batch, seq, D_MODEL)
```

```python
import functools
import math

import jax
import jax.numpy as jnp
import numpy as np
from jax import lax
from jax.experimental import pallas as pl
from jax.experimental.pallas import tpu as pltpu

F32 = jnp.float32
BF16 = jnp.bfloat16

D_MODEL = 1024
GRID_W = 64
ROPE_BASE = 10000.0
EPS = 1e-6
NEG_INF = -1e30
LOG2E = math.log2(math.e)
KEY_CHUNK = 512
LANES = 128

MLA_HEADS, MLA_Q_RANK, MLA_KV_RANK, MLA_NOPE, MLA_ROPE, MLA_V = 8, 256, 128, 64, 32, 64
SWA_HEADS, SWA_KV_HEADS, SWA_DIM, SWA_WINDOW = 8, 2, 64, 128
NA_HEADS, NA_DIM, NA_KH, NA_KW = 8, 64, 8, 16
GQA_HEADS, GQA_KV_HEADS, GQA_DIM = 4, 2, 128
D_FF = 2816
FF_CHUNK = 256
N_FF_CHUNKS = D_FF // FF_CHUNK
HALO = 16

NA_TILE_ROWS = 4
NA_WIN_ROWS = NA_TILE_ROWS + NA_KH - 1

VMEM_LIMIT = 56 * 1024 * 1024

TILE_MIX_FFN = 512
TILE_MLA_Q = 1024
TILE_GQA_Q = 1024
TILE_SWA_Q = 512


def _params():
    return pltpu.CompilerParams(vmem_limit_bytes=VMEM_LIMIT)


def _dot(a, b):
    return jnp.dot(a, b, preferred_element_type=F32)


def _dot_nt(a, b):
    return lax.dot_general(a, b, (((1,), (1,)), ((), ())), preferred_element_type=F32)


def _rms(x, g):
    return x * lax.rsqrt(jnp.mean(x * x, axis=-1, keepdims=True) + EPS) * g


def _rope(x, c, sa, sb, sha, shb):
    return x * c + pltpu.roll(x, sha, axis=1) * sa + pltpu.roll(x, shb, axis=1) * sb


def _lane_lo(shape):
    return lax.broadcasted_iota(jnp.int32, shape, len(shape) - 1) < (LANES // 2)


def _const_spec(shape):
    nd = len(shape)
    return pl.BlockSpec(shape, lambda *_: (0,) * nd, pipeline_mode=pl.Buffered(1))


def _ada_kernel(cc_ref, w_ref, b_ref, o_ref):
    cc = cc_ref[...]
    s = cc * (1.0 / (1.0 + jnp.exp(-cc)))
    o_ref[0] = _dot(s.astype(BF16), w_ref[0].astype(BF16)) + b_ref[0]


def _ada(cc, ada_w, ada_b):
    depth = ada_w.shape[0]
    rows = cc.shape[0]
    return pl.pallas_call(
        _ada_kernel,
        grid=(depth, 6),
        in_specs=[
            pl.BlockSpec((rows, D_MODEL), lambda l, j: (0, 0)),
            pl.BlockSpec((1, D_MODEL, D_MODEL), lambda l, j: (l, 0, j)),
            pl.BlockSpec((1, 1, D_MODEL), lambda l, j: (l, 0, j)),
        ],
        out_specs=pl.BlockSpec((1, rows, D_MODEL), lambda l, j: (l, 0, j)),
        out_shape=jax.ShapeDtypeStruct((depth, rows, 6 * D_MODEL), F32),
        compiler_params=_params(),
        name="ada",
    )(cc, ada_w, ada_b.reshape(depth, 1, 6 * D_MODEL))


class _Rows:
    def __init__(self, n, seq, tm, per_batch_mod):
        assert seq % tm == 0 and n % seq == 0
        self.n, self.seq, self.tm = n, seq, tm
        self.tps = seq // tm
        self.grid = (n // tm,)
        self.per_batch_mod = per_batch_mod

    def row_spec(self, width):
        return pl.BlockSpec((self.tm, width), lambda i: (i, 0))

    def mod_spec(self):
        if self.per_batch_mod:
            tps = self.tps
            return pl.BlockSpec((1, 8, D_MODEL), lambda i: (i // tps, 0, 0))
        return pl.BlockSpec((1, 8, D_MODEL), lambda i: (0, 0, 0))

    def table_spec(self, width):
        tps = self.tps
        return pl.BlockSpec((self.tm, width), lambda i: (i % tps, 0))


PRE_HALF = 512


def _row_halves(tm):
    half = min(tm, PRE_HALF)
    assert tm % half == 0
    return [slice(r, r + half) for r in range(0, tm, half)]


def _modulated(x_ref, mod_ref, g_ref, rows, shift_row, scale_row):
    x = x_ref[rows, :]
    shift = mod_ref[0, shift_row:shift_row + 1, :]
    scale = mod_ref[0, scale_row:scale_row + 1, :]
    return _rms(x, g_ref[...]) * (1.0 + scale) + shift


EVEN_COLS = 1280


def _pre_even_kernel(x_ref, mod_ref, g_ref, win_ref, qg_ref, wuq_ref, kvg_ref, wuk_ref, wuv_ref,
                     tab_ref, qm_ref, km_ref, vm_ref, qs_ref, ks_ref, vs_ref):
    mla_scale = LOG2E * (MLA_NOPE + MLA_ROPE) ** -0.5
    swa_scale = LOG2E * SWA_DIM ** -0.5
    for rows in _row_halves(x_ref.shape[0]):
        h = _modulated(x_ref, mod_ref, g_ref, rows, 0, 1).astype(BF16)
        mc, ma, mb = tab_ref[rows, 0:128], tab_ref[rows, 128:256], tab_ref[rows, 256:384]
        sc, sa, sb = tab_ref[rows, 384:512], tab_ref[rows, 512:640], tab_ref[rows, 640:768]

        pa = _dot(h, win_ref[:, 0:512])
        cq = _rms(pa[:, 0:256], qg_ref[...]).astype(BF16)
        ckv = _rms(pa[:, 256:384], kvg_ref[...]).astype(BF16)
        kpe = _rope(pa[:, 384:512], mc, ma, mb, 112, 16)
        q = _dot(cq, wuq_ref[...])
        kn = _dot(ckv, wuk_ref[...])
        for j in range(MLA_HEADS):
            sl = slice(j * LANES, (j + 1) * LANES)
            qm_ref[rows, sl] = (_rope(q[:, sl], mc, ma, mb, 112, 16) * mla_scale).astype(BF16)
            km_ref[rows, sl] = (kn[:, sl] + kpe).astype(BF16)
        vm_ref[rows, :] = _dot(ckv, wuv_ref[...]).astype(BF16)

        pq = _dot(h, win_ref[:, 512:1024])
        for j in range(4):
            sl = slice(j * LANES, (j + 1) * LANES)
            qs_ref[rows, sl] = (_rope(pq[:, sl], sc, sa, sb, 96, 32) * swa_scale).astype(BF16)
        pkv = _dot(h, win_ref[:, 1024:1280])
        ks_ref[rows, :] = _rope(pkv[:, 0:128], sc, sa, sb, 96, 32).astype(BF16)
        vs_ref[rows, :] = pkv[:, 128:256].astype(BF16)


def _pre_even(rows, x, mod, g, win, qg, wuq, kvg, wuk, wuv, tab):
    n = rows.n
    widths = (1024, 1024, 512, 512, 128, 128)
    return pl.pallas_call(
        _pre_even_kernel,
        grid=rows.grid,
        in_specs=[
            rows.row_spec(D_MODEL), rows.mod_spec(), _const_spec((1, D_MODEL)),
            _const_spec(win.shape), _const_spec(qg.shape), _const_spec(wuq.shape),
            _const_spec(kvg.shape), _const_spec(wuk.shape), _const_spec(wuv.shape),
            rows.table_spec(768),
        ],
        out_specs=[rows.row_spec(w) for w in widths],
        out_shape=[jax.ShapeDtypeStruct((n, w), BF16) for w in widths],
        compiler_params=_params(),
        name="pre_even",
    )(x, mod, g, win, qg, wuq, kvg, wuk, wuv, tab)


def _pre_odd_kernel(x_ref, mod_ref, g_ref, win_ref, qg_ref, kg_ref, tab_ref,
                    qn_ref, kn_ref, vn_ref, qq_ref, kq_ref, vq_ref):
    na_scale = LOG2E * NA_DIM ** -0.5
    gqa_scale = LOG2E * GQA_DIM ** -0.5
    for rows in _row_halves(x_ref.shape[0]):
        h = _modulated(x_ref, mod_ref, g_ref, rows, 0, 1).astype(BF16)
        gc, gs = tab_ref[rows, 0:128], tab_ref[rows, 128:256]

        def normed(blk, gain, gc=gc, gs=gs):
            y = _rms(blk, gain)
            return y * gc + pltpu.roll(y, 64, axis=1) * gs

        qn_ref[rows, :] = (_dot(h, win_ref[:, 0:512]) * na_scale).astype(BF16)
        kn_ref[rows, :] = _dot(h, win_ref[:, 512:1024]).astype(BF16)
        vn_ref[rows, :] = _dot(h, win_ref[:, 1024:1536]).astype(BF16)
        pq = _dot(h, win_ref[:, 1536:2048])
        for j in range(GQA_HEADS):
            sl = slice(j * LANES, (j + 1) * LANES)
            qq_ref[rows, sl] = (normed(pq[:, sl], qg_ref[...]) * gqa_scale).astype(BF16)
        pkv = _dot(h, win_ref[:, 2048:2560])
        for j in range(GQA_KV_HEADS):
            sl = slice(j * LANES, (j + 1) * LANES)
            kq_ref[rows, sl] = normed(pkv[:, sl], kg_ref[...]).astype(BF16)
        vq_ref[rows, :] = pkv[:, 256:512].astype(BF16)


def _pre_odd(rows, x, mod, g, win, qg, kg, tab):
    n = rows.n
    widths = (512, 512, 512, 512, 256, 256)
    return pl.pallas_call(
        _pre_odd_kernel,
        grid=rows.grid,
        in_specs=[
            rows.row_spec(D_MODEL), rows.mod_spec(), _const_spec((1, D_MODEL)),
            _const_spec(win.shape), _const_spec(qg.shape), _const_spec(kg.shape),
            rows.table_spec(256),
        ],
        out_specs=[rows.row_spec(w) for w in widths],
        out_shape=[jax.ShapeDtypeStruct((n, w), BF16) for w in widths],
        compiler_params=_params(),
        name="pre_odd",
    )(x, mod, g, win, qg, kg, tab)


def _attend(q, chunks, sink=None):
    return _attend_many([(q, chunks, sink)])[0]


def _attend_many(streams):
    state = []
    for q, _, sink in streams:
        if sink is None:
            state.append((None, None))
        else:
            rows = q.shape[0]
            state.append((sink, jnp.concatenate([jnp.zeros((rows, LANES), F32), jnp.ones((rows, LANES), F32)],
                                                axis=1)))
    for c in range(max(len(chunks) for _, chunks, _ in streams)):
        for idx, (q, chunks, _) in enumerate(streams):
            if c >= len(chunks):
                continue
            k, v, bias = chunks[c]
            m, acc = state[idx]
            s = _dot_nt(q, k)
            if bias is not None:
                s = s + bias
            cm = jnp.max(s, axis=-1, keepdims=True)
            m_new = cm if m is None else jnp.maximum(m, cm)
            p = jnp.exp2(s - m_new).astype(BF16)
            vx = jnp.concatenate([v, jnp.ones(v.shape, BF16)], axis=1)
            pv = _dot(p, vx)
            acc = pv if acc is None else jnp.exp2(m - m_new) * acc + pv
            state[idx] = (m_new, acc)
    return [acc[:, :LANES] / acc[:, LANES:] for _, acc in state]


def _chunks(k_ref, v_ref, lanes, size):
    n = k_ref.shape[0]
    assert n % size == 0
    return [(k_ref[c * size:(c + 1) * size, lanes], v_ref[c * size:(c + 1) * size, :], None)
            for c in range(n // size)]


def _mla_kernel(has_lat, q_ref, kc_ref, vc_ref, *rest):
    if has_lat:
        kl_ref, vl_ref, o_ref = rest
    else:
        (o_ref,) = rest
    streams = []
    n_pairs = o_ref.shape[1] // LANES
    for e in range(2 * n_pairs):
        sl = slice(e * LANES, (e + 1) * LANES)
        vl_lanes = slice((e // 2) * LANES, (e // 2 + 1) * LANES)
        chunks = [(kc_ref[:, sl], vc_ref[:, vl_lanes], None)]
        if has_lat:
            chunks += [(k, v[:, vl_lanes], b) for k, v, b in _chunks(kl_ref, vl_ref, sl, KEY_CHUNK)]
        streams.append((q_ref[:, sl], chunks, None))
    outs = _attend_many(streams)
    lo = _lane_lo(outs[0].shape)
    for j in range(n_pairs):
        o_ref[:, j * LANES:(j + 1) * LANES] = jnp.where(lo, outs[2 * j], outs[2 * j + 1]).astype(BF16)


def _mla_attn(q, kc, vc, kl, vl, batch, q_seq, ctx_len, lat_seq, tq):
    has_lat = kl is not None
    nq = q_seq // tq
    pairs = 1 if has_lat else MLA_HEADS // 2
    in_specs = [
        pl.BlockSpec((tq, 256 * pairs), lambda b, j, t: (b * nq + t, j)),
        pl.BlockSpec((ctx_len, 256 * pairs), lambda b, j, t: (b, j)),
        pl.BlockSpec((ctx_len, 128 * pairs), lambda b, j, t: (b, j)),
    ]
    args = [q, kc, vc]
    if has_lat:
        in_specs += [
            pl.BlockSpec((lat_seq, 256), lambda b, j, t: (b, j)),
            pl.BlockSpec((lat_seq, 128), lambda b, j, t: (b, j)),
        ]
        args += [kl, vl]
    return pl.pallas_call(
        functools.partial(_mla_kernel, has_lat),
        grid=(batch, MLA_HEADS // 2 // pairs, nq),
        in_specs=in_specs,
        out_specs=pl.BlockSpec((tq, 128 * pairs), lambda b, j, t: (b * nq + t, j)),
        out_shape=jax.ShapeDtypeStruct((batch * q_seq, MLA_HEADS * MLA_V), BF16),
        compiler_params=_params(),
        name="mla_attn_lat" if has_lat else "mla_attn_ctx",
    )(*args)


def _gqa_kernel(q_ref, kc_ref, vc_ref, kl_ref, vl_ref, o_ref):
    tq = q_ref.shape[0]
    per_group = GQA_HEADS // GQA_KV_HEADS
    q = jnp.concatenate([q_ref[:, e * LANES:(e + 1) * LANES] for e in range(per_group)], axis=0)
    every = slice(None)
    chunks = _chunks(kc_ref, vc_ref, every, kc_ref.shape[0]) + _chunks(kl_ref, vl_ref, every, KEY_CHUNK)
    o = _attend(q, chunks).astype(BF16)
    for e in range(per_group):
        o_ref[:, e * LANES:(e + 1) * LANES] = o[e * tq:(e + 1) * tq]


def _gqa_attn(q, kc, vc, kl, vl, batch, seq, ctx_len, tq):
    nq = seq // tq
    return pl.pallas_call(
        _gqa_kernel,
        grid=(batch, GQA_KV_HEADS, nq),
        in_specs=[
            pl.BlockSpec((tq, 256), lambda b, g, t: (b * nq + t, g)),
            pl.BlockSpec((ctx_len, 128), lambda b, g, t: (b, g)),
            pl.BlockSpec((ctx_len, 128), lambda b, g, t: (b, g)),
            pl.BlockSpec((seq, 128), lambda b, g, t: (b, g)),
            pl.BlockSpec((seq, 128), lambda b, g, t: (b, g)),
        ],
        out_specs=pl.BlockSpec((tq, 256), lambda b, g, t: (b * nq + t, g)),
        out_shape=jax.ShapeDtypeStruct((batch * seq, GQA_HEADS * GQA_DIM), BF16),
        compiler_params=_params(),
        name="gqa_attn",
    )(q, kc, vc, kl, vl)


SWA_QB = 128
SWA_GROUP = SWA_HEADS // SWA_KV_HEADS


def _swa_group_lhs(q_ref, r0, rows, g):
    lo = _lane_lo((rows, LANES))
    keep = lo if g == 0 else jnp.logical_not(lo)
    blocks = [jnp.where(keep, q_ref[r0:r0 + rows, j * LANES:(j + 1) * LANES], jnp.zeros((), BF16))
              for j in range(SWA_GROUP)]
    return jnp.concatenate(blocks, axis=0)


def _swa_sink_col(sink_ref, g, rows):
    blk = lax.broadcasted_iota(jnp.int32, (SWA_GROUP * rows, 1), 0) // rows
    col = jnp.full((SWA_GROUP * rows, 1), sink_ref[g * SWA_GROUP], F32)
    for j in range(1, SWA_GROUP):
        col = jnp.where(blk == j, sink_ref[g * SWA_GROUP + j], col)
    return col * LOG2E


def _swa_store(o_ref, r0, rows, o_groups):
    lo = _lane_lo((rows, LANES))
    for j in range(SWA_GROUP):
        a = o_groups[0][j * rows:(j + 1) * rows]
        b = o_groups[1][j * rows:(j + 1) * rows]
        o_ref[r0:r0 + rows, j * LANES:(j + 1) * LANES] = jnp.where(lo, a, b).astype(BF16)


SWA_BAND = SWA_QB + 2 * SWA_WINDOW


def _swa_mask_table():
    ii = np.arange(SWA_GROUP * SWA_QB)[:, None] % SWA_QB
    jj = np.arange(SWA_BAND)[None, :]
    in_window = np.abs(jj - SWA_WINDOW - ii) <= SWA_WINDOW
    first = in_window & (jj >= SWA_WINDOW)
    last = in_window & (jj < SWA_WINDOW + SWA_QB)
    return jnp.asarray(np.where(np.stack([first, in_window, last]), 0.0, NEG_INF), F32)


def _swa_lat_kernel(tq, sink_ref, mask_ref, q_ref, kp_ref, kt_ref, kn_ref, vp_ref, vt_ref, vn_ref,
                    kc_ref, vc_ref, o_ref):
    t = pl.program_id(1)
    kband = jnp.concatenate([kp_ref[...], kt_ref[...], kn_ref[...]], axis=0)
    vband = jnp.concatenate([vp_ref[...], vt_ref[...], vn_ref[...]], axis=0)
    band = SWA_BAND
    n_blocks = tq // SWA_QB
    for i in range(n_blocks):
        variant = 1
        if i == 0:
            variant = jnp.where(t == 0, 0, variant)
        if i == n_blocks - 1:
            variant = jnp.where(t == pl.num_programs(1) - 1, 2, variant)
        bias = mask_ref[variant]
        kb = kband[i * SWA_QB:i * SWA_QB + band]
        vb = vband[i * SWA_QB:i * SWA_QB + band]
        o_groups = []
        for g in range(SWA_KV_HEADS):
            lhs = _swa_group_lhs(q_ref, i * SWA_QB, SWA_QB, g)
            o_groups.append(_attend(lhs, [(kc_ref[...], vc_ref[...], None), (kb, vb, bias)],
                                    _swa_sink_col(sink_ref, g, SWA_QB)))
        _swa_store(o_ref, i * SWA_QB, SWA_QB, o_groups)


def _swa_ctx_kernel(ctx_len, sink_ref, q_ref, kc_ref, vc_ref, o_ref):
    o_groups = []
    for g in range(SWA_KV_HEADS):
        lhs = _swa_group_lhs(q_ref, 0, ctx_len, g)
        o_groups.append(_attend(lhs, [(kc_ref[...], vc_ref[...], None)], _swa_sink_col(sink_ref, g, ctx_len)))
    _swa_store(o_ref, 0, ctx_len, o_groups)


def _swa_attn_lat(sink, q, k, v, kc, vc, batch, seq, ctx_len, tq):
    nq = seq // tq
    nb = seq // SWA_WINDOW
    per = tq // SWA_WINDOW
    prev_spec = pl.BlockSpec((SWA_WINDOW, 128), lambda b, t: (b * nb + jnp.maximum(t * per - 1, 0), 0))
    next_spec = pl.BlockSpec((SWA_WINDOW, 128), lambda b, t: (b * nb + jnp.minimum(t * per + per, nb - 1), 0))
    tile_spec = pl.BlockSpec((tq, 128), lambda b, t: (b * nq + t, 0))
    ctx_spec = pl.BlockSpec((ctx_len, 128), lambda b, t: (b, 0))
    mask = _swa_mask_table()
    return pl.pallas_call(
        functools.partial(_swa_lat_kernel, tq),
        grid=(batch, nq),
        in_specs=[
            pl.BlockSpec(memory_space=pltpu.SMEM),
            pl.BlockSpec(mask.shape, lambda b, t: (0, 0, 0)),
            pl.BlockSpec((tq, 512), lambda b, t: (b * nq + t, 0)),
            prev_spec, tile_spec, next_spec, prev_spec, tile_spec, next_spec,
            ctx_spec, ctx_spec,
        ],
        out_specs=pl.BlockSpec((tq, 512), lambda b, t: (b * nq + t, 0)),
        out_shape=jax.ShapeDtypeStruct((batch * seq, SWA_HEADS * SWA_DIM), BF16),
        compiler_params=_params(),
        name="swa_attn_lat",
    )(sink, mask, q, k, k, k, v, v, v, kc, vc)


def _swa_attn_ctx(sink, q, kc, vc, batch, ctx_len):
    return pl.pallas_call(
        functools.partial(_swa_ctx_kernel, ctx_len),
        grid=(batch,),
        in_specs=[
            pl.BlockSpec(memory_space=pltpu.SMEM),
            pl.BlockSpec((ctx_len, 512), lambda b: (b, 0)),
            pl.BlockSpec((ctx_len, 128), lambda b: (b, 0)),
            pl.BlockSpec((ctx_len, 128), lambda b: (b, 0)),
        ],
        out_specs=pl.BlockSpec((ctx_len, 512), lambda b: (b, 0)),
        out_shape=jax.ShapeDtypeStruct((batch * ctx_len, SWA_HEADS * SWA_DIM), BF16),
        compiler_params=_params(),
        name="swa_attn_ctx",
    )(sink, q, kc, vc)


NA_TILES_PER_STEP = 8


def _na_kernel(n_tiles, tab_ref, q_ref, k_ref, v_ref, kc_ref, vc_ref, o_ref):
    grid_rows = n_tiles * NA_TILE_ROWS
    tq = NA_TILE_ROWS * GRID_W
    lo = _lane_lo((tq, LANES))
    for u in range(NA_TILES_PER_STEP):
        t = pl.program_id(2) * NA_TILES_PER_STEP + u
        ws = jnp.clip(t * NA_TILE_ROWS - NA_KH // 2, 0, grid_rows - NA_WIN_ROWS)
        variant = jnp.where(t == 0, 0, jnp.where(t == n_tiles - 1, 2, 1))
        start = pl.multiple_of(ws * GRID_W, GRID_W)
        kwin = k_ref[pl.ds(start, NA_WIN_ROWS * GRID_W), :]
        vwin = v_ref[pl.ds(start, NA_WIN_ROWS * GRID_W), :]
        q = q_ref[u * tq:(u + 1) * tq, :]
        zero = jnp.zeros((), BF16)
        lhs = jnp.concatenate([jnp.where(lo, q, zero), jnp.where(lo, zero, q)], axis=0)
        bias = tab_ref[variant].reshape(2 * tq, NA_WIN_ROWS * GRID_W)
        o = _attend(lhs, [(kc_ref[...], vc_ref[...], None), (kwin, vwin, bias)])
        o_ref[u * tq:(u + 1) * tq, :] = jnp.where(lo, o[:tq], o[tq:]).astype(BF16)


def _na_attn(tab, q, k, v, kc, vc, batch, seq, ctx_len):
    tq = NA_TILE_ROWS * GRID_W
    n_tiles = seq // tq
    nk = NA_WIN_ROWS * GRID_W
    assert n_tiles % NA_TILES_PER_STEP == 0
    n_steps = n_tiles // NA_TILES_PER_STEP
    tq_step = tq * NA_TILES_PER_STEP
    return pl.pallas_call(
        functools.partial(_na_kernel, n_tiles),
        grid=(NA_HEADS // 2, batch, n_steps),
        in_specs=[
            pl.BlockSpec((3, 2, tq, nk), lambda j, b, t: (0, j, 0, 0)),
            pl.BlockSpec((tq_step, 128), lambda j, b, t: (b * n_steps + t, j)),
            pl.BlockSpec((seq, 128), lambda j, b, t: (b, j)),
            pl.BlockSpec((seq, 128), lambda j, b, t: (b, j)),
            pl.BlockSpec((ctx_len, 128), lambda j, b, t: (b, j)),
            pl.BlockSpec((ctx_len, 128), lambda j, b, t: (b, j)),
        ],
        out_specs=pl.BlockSpec((tq_step, 128), lambda j, b, t: (b * n_steps + t, j)),
        out_shape=jax.ShapeDtypeStruct((batch * seq, NA_HEADS * NA_DIM), BF16),
        compiler_params=_params(),
        name="na_attn",
    )(tab, q, k, v, kc, vc)


def _na_bias_table(rpb, grid_rows):
    cols = np.arange(GRID_W)
    col_start = np.clip(cols - NA_KW // 2, 0, GRID_W - NA_KW)
    col_ok = (cols[None, :] >= col_start[:, None]) & (cols[None, :] < col_start[:, None] + NA_KW)
    dc = np.clip(cols[None, :] - cols[:, None], -(NA_KW - 1), NA_KW - 1) + NA_KW - 1
    n_heads, n_dr = rpb.shape[0], rpb.shape[1]
    by_col = jnp.take(rpb.astype(F32) * LOG2E, jnp.asarray(dc.reshape(-1)), axis=2).reshape(n_heads, n_dr, GRID_W, GRID_W)
    by_col = jnp.where(jnp.asarray(col_ok)[None, None], by_col, NEG_INF)
    by_col = jnp.concatenate([by_col, jnp.full((n_heads, 1, GRID_W, GRID_W), NEG_INF, F32)], axis=1)
    n_tiles = grid_rows // NA_TILE_ROWS
    variants = []
    for t in (0, min(2, n_tiles - 1), n_tiles - 1):
        r0 = t * NA_TILE_ROWS
        ws = int(np.clip(r0 - NA_KH // 2, 0, grid_rows - NA_WIN_ROWS))
        r = r0 + np.arange(NA_TILE_ROWS)
        rs = np.clip(r - NA_KH // 2, 0, grid_rows - NA_KH)
        key_row = ws + np.arange(NA_WIN_ROWS)
        row_ok = (key_row[None, :] >= rs[:, None]) & (key_row[None, :] < rs[:, None] + NA_KH)
        dr = np.where(row_ok, key_row[None, :] - r[:, None] + NA_KH - 1, n_dr)
        rows = []
        for i in range(NA_TILE_ROWS):
            rows.append(jnp.concatenate([by_col[:, int(d)] for d in dr[i]], axis=-1))
        variants.append(jnp.concatenate(rows, axis=1))
    return jnp.stack(variants, axis=0)


def _gelu_tanh(x):
    return x * (0.5 * (1.0 + jnp.tanh(math.sqrt(2.0 / math.pi) * (x + 0.044715 * (x * x * x)))))


def _mix_ffn_kernel(tm, tps, a_ref, b_ref, x_ref, modp_ref, modf_ref, wa_ref, wb_ref, g1_ref, g2_ref, g3_ref,
                    wg_ref, wv_ref, cw_ref, wd_ref, o_ref, hh_ref, x1_ref, ug_ref, uv_ref, act_ref, acc_ref):
    i = pl.program_id(0)
    zero = jnp.zeros((), BF16)

    @pl.when(i == 0)
    def _():
        hh_ref[...] = jnp.zeros_like(hh_ref)
        x1_ref[...] = jnp.zeros_like(x1_ref)

    def step(s):
        o = 1 - s
        y = _dot(a_ref[...], wa_ref[...]) + _dot(b_ref[...], wb_ref[...])
        x1 = x_ref[...] + modp_ref[0, 2:3, :] * _rms(y, g1_ref[...])
        h = (_rms(x1, g2_ref[...]) * (1.0 + modp_ref[0, 4:5, :]) + modp_ref[0, 3:4, :]).astype(BF16)
        x1_ref[s] = x1
        hh_ref[s, HALO:HALO + tm] = h
        first = i % tps == 0
        hh_ref[s, 0:HALO] = jnp.where(first, zero, hh_ref[o, tm:tm + HALO])
        hh_ref[o, HALO + tm:] = jnp.where(first, zero, h[0:HALO])

        def up(c, slot):
            uv_ref[slot] = _dot(hh_ref[o, HALO:HALO + tm], wv_ref[c])
            ug_ref[slot] = _dot(hh_ref[o], wg_ref[c])

        def activate(c, slot):
            cw = cw_ref[c]
            gate = cw[3:4]
            for tap in range(3):
                gate = gate + ug_ref[slot, HALO - 1 + tap:HALO - 1 + tap + tm] * cw[tap:tap + 1]
            act_ref[c] = (_gelu_tanh(gate) * uv_ref[slot]).astype(BF16)

        up(0, 0)
        for c in range(N_FF_CHUNKS):
            if c + 1 < N_FF_CHUNKS:
                up(c + 1, (c + 1) % 2)
            activate(c, c % 2)

        for nb in range(D_MODEL // FF_CHUNK):
            cols = slice(nb * FF_CHUNK, (nb + 1) * FF_CHUNK)
            f = _dot(act_ref[0], wd_ref[0, :, cols])
            for c in range(1, N_FF_CHUNKS):
                f = f + _dot(act_ref[c], wd_ref[c, :, cols])
            acc_ref[:, cols] = f
        o_ref[...] = x1_ref[o] + modf_ref[0, 5:6, :] * _rms(acc_ref[...], g3_ref[...])

    pl.when(i % 2 == 0)(lambda: step(0))
    pl.when(i % 2 == 1)(lambda: step(1))


def _mix_ffn(rows, a, b, x, mod, wa, wb, g1, g2, g3, wg, wv, cw, wd):
    n, tm, tps = rows.n, rows.tm, rows.tps
    n_tiles = n // tm
    assert n_tiles % tps == 0

    def tile_in(width):
        return pl.BlockSpec((tm, width), lambda i: (jnp.minimum(i, n_tiles - 1), 0))

    def mod_spec(tile_of_step):
        if rows.per_batch_mod:
            return pl.BlockSpec((1, 8, D_MODEL), lambda i: (tile_of_step(i) // tps, 0, 0))
        return pl.BlockSpec((1, 8, D_MODEL), lambda i: (0, 0, 0))

    row = _const_spec((1, D_MODEL))
    return pl.pallas_call(
        functools.partial(_mix_ffn_kernel, tm, tps),
        grid=(n_tiles + 1,),
        in_specs=[
            tile_in(a.shape[1]), tile_in(b.shape[1]), tile_in(D_MODEL),
            mod_spec(lambda i: jnp.minimum(i, n_tiles - 1)), mod_spec(lambda i: jnp.maximum(i - 1, 0)),
            _const_spec(wa.shape), _const_spec(wb.shape), row, row, row,
            _const_spec(wg.shape), _const_spec(wv.shape), _const_spec(cw.shape), _const_spec(wd.shape),
        ],
        out_specs=pl.BlockSpec((tm, D_MODEL), lambda i: (jnp.maximum(i - 1, 0), 0)),
        out_shape=jax.ShapeDtypeStruct((n, D_MODEL), F32),
        scratch_shapes=[
            pltpu.VMEM((2, tm + 2 * HALO, D_MODEL), BF16),
            pltpu.VMEM((2, tm, D_MODEL), F32),
            pltpu.VMEM((2, tm + 2 * HALO, FF_CHUNK), F32),
            pltpu.VMEM((2, tm, FF_CHUNK), F32),
            pltpu.VMEM((N_FF_CHUNKS, tm, FF_CHUNK), BF16),
            pltpu.VMEM((tm, D_MODEL), F32),
        ],
        compiler_params=_params(),
        name="mix_ffn",
    )(a, b, x, mod, mod, wa, wb, g1, g2, g3, wg, wv, cw, wd)


def _rope_angles(n, d):
    t = jnp.arange(n)
    rows = (t // GRID_W).astype(F32)
    cols = (t % GRID_W).astype(F32)
    freqs = ROPE_BASE ** (-jnp.arange(d // 4, dtype=F32) * 4.0 / d)
    ang = jnp.concatenate([rows[:, None] * freqs, cols[:, None] * freqs], axis=-1)
    return jnp.cos(ang), jnp.sin(ang)


def _rope_lane_tables(cos, sin, offset, group):
    n, half = cos.shape
    c = jnp.ones((n, group), F32)
    sa = jnp.zeros((n, group), F32)
    sb = jnp.zeros((n, group), F32)
    c = c.at[:, offset:offset + half].set(cos).at[:, offset + half:offset + 2 * half].set(cos)
    sa = sa.at[:, offset:offset + half].set(-sin)
    sb = sb.at[:, offset + half:offset + 2 * half].set(sin)
    reps = LANES // group
    return tuple(jnp.tile(a, (1, reps)) for a in (c, sa, sb))


def _identity_tables(n, width_blocks):
    ones = jnp.ones((n, LANES), F32)
    zeros = jnp.zeros((n, LANES), F32)
    return [ones, zeros, zeros][:width_blocks]


def _even_weights(w_in, w_out, w_uq, w_ukv):
    z = lambda r, c: jnp.zeros((r, c), F32)
    mla_in = MLA_Q_RANK + MLA_KV_RANK + MLA_ROPE
    kpe = w_in[:, MLA_Q_RANK + MLA_KV_RANK:mla_in]
    swa = w_in[:, mla_in:]
    swa_q = swa[:, :SWA_HEADS * SWA_DIM].reshape(D_MODEL, SWA_HEADS, SWA_DIM)
    pair_order = np.array([h for j in range(SWA_GROUP) for h in (j, SWA_GROUP + j)])
    swa_q = swa_q[:, pair_order, :].reshape(D_MODEL, SWA_HEADS * SWA_DIM)
    win = jnp.concatenate([
        w_in[:, :MLA_Q_RANK + MLA_KV_RANK], z(D_MODEL, 64), kpe, z(D_MODEL, 32), swa_q,
        swa[:, SWA_HEADS * SWA_DIM:]], axis=1)
    assert win.shape[1] == EVEN_COLS
    wuq = w_uq.reshape(MLA_Q_RANK, MLA_HEADS, MLA_NOPE + MLA_ROPE)
    wuq = jnp.pad(wuq, ((0, 0), (0, 0), (0, LANES - MLA_NOPE - MLA_ROPE))).reshape(MLA_Q_RANK, MLA_HEADS * LANES)
    wukv = w_ukv.reshape(MLA_KV_RANK, MLA_HEADS, MLA_NOPE + MLA_V)
    wuk = jnp.pad(wukv[:, :, :MLA_NOPE], ((0, 0), (0, 0), (0, LANES - MLA_NOPE))).reshape(MLA_KV_RANK, MLA_HEADS * LANES)
    wuv = wukv[:, :, MLA_NOPE:].reshape(MLA_KV_RANK, MLA_HEADS * MLA_V)
    n_a = MLA_HEADS * MLA_V
    wo_a = w_out[:n_a]
    wo_b = w_out[n_a:].reshape(SWA_HEADS, SWA_DIM, D_MODEL)[pair_order].reshape(SWA_HEADS * SWA_DIM, D_MODEL)
    return tuple(a.astype(BF16) for a in (win, wuq, wuk, wuv, wo_a, wo_b))


def _ffn_weights(w_up, conv_w, conv_b, w_down):
    def chunks(w):
        return w.reshape(D_MODEL, N_FF_CHUNKS, FF_CHUNK).transpose(1, 0, 2).astype(BF16)

    cw = jnp.concatenate([conv_w, conv_b[None], jnp.zeros((4, D_FF), F32)], axis=0)
    cw = cw.reshape(8, N_FF_CHUNKS, FF_CHUNK).transpose(1, 0, 2)
    wd = w_down.reshape(N_FF_CHUNKS, FF_CHUNK, D_MODEL).astype(BF16)
    return chunks(w_up[:, :D_FF]), chunks(w_up[:, D_FF:]), cw, wd


def kernel(x, c, ctx, c_ctx, ada_w, ada_b, pre_mix_g, post_mix_g, pre_ffn_g, post_ffn_g, ffn_up, ffn_conv_w,
           ffn_conv_b, ffn_down, even_w_in, even_w_out, mla_q_norm_g, mla_w_uq, mla_kv_norm_g, mla_w_ukv,
           swa_sink, odd_w_in, odd_w_out, na_rpb, gqa_q_g, gqa_k_g):
    batch, seq, _ = x.shape
    ctx_len = ctx.shape[1]
    depth = ada_w.shape[0]
    n_lat, n_ctx = batch * seq, batch * ctx_len

    pad_rows = -(batch + 1) % 8
    cc = jnp.concatenate([c, c_ctx[None], jnp.zeros((pad_rows, D_MODEL), F32)], axis=0)
    mods = _ada(cc, ada_w, ada_b).reshape(depth, cc.shape[0], 6, D_MODEL)
    mods = jnp.pad(mods, ((0, 0), (0, 0), (0, 2), (0, 0)))

    lat_rows = _Rows(n_lat, seq, TILE_MIX_FFN, True)
    pre_rows = _Rows(n_lat, seq, 2 * PRE_HALF, True)
    ctx_rows = _Rows(n_ctx, ctx_len, ctx_len, False)

    cos_m, sin_m = _rope_angles(seq, MLA_ROPE)
    cos_s, sin_s = _rope_angles(seq, SWA_DIM)
    cos_g, sin_g = _rope_angles(seq, GQA_DIM)
    gc, gsa, gsb = _rope_lane_tables(cos_g, sin_g, 0, LANES)
    tab_even = jnp.concatenate(_rope_lane_tables(cos_m, sin_m, MLA_NOPE, LANES)
                               + _rope_lane_tables(cos_s, sin_s, 0, SWA_DIM), axis=1)
    tab_odd = jnp.concatenate([gc, gsa + gsb], axis=1)
    tab_even_ctx = jnp.concatenate(_identity_tables(ctx_len, 3) * 2, axis=1)
    tab_odd_ctx = jnp.concatenate(_identity_tables(ctx_len, 2), axis=1)

    x = x.reshape(n_lat, D_MODEL)
    ctx = ctx.reshape(n_ctx, D_MODEL)
    row = lambda v: v.reshape(1, -1)

    for layer in range(depth):
        need_ctx = layer < depth - 1
        i = layer // 2
        mod_lat = mods[layer, :batch]
        mod_ctx = mods[layer, batch:batch + 1]
        if layer % 2 == 0:
            win, wuq, wuk, wuv, wo_a, wo_b = _even_weights(even_w_in[i], even_w_out[i], mla_w_uq[i], mla_w_ukv[i])
            shared = (row(pre_mix_g[layer]), win, row(mla_q_norm_g[i]), wuq, row(mla_kv_norm_g[i]), wuk, wuv)
            qm, km, vm, qs, ks, vs = _pre_even(pre_rows, x, mod_lat, *shared, tab_even)
            qmc, kmc, vmc, qsc, ksc, vsc = _pre_even(ctx_rows, ctx, mod_ctx, *shared, tab_even_ctx)
            a_lat = _mla_attn(qm, kmc, vmc, km, vm, batch, seq, ctx_len, seq, TILE_MLA_Q)
            b_lat = _swa_attn_lat(swa_sink[i], qs, ks, vs, ksc, vsc, batch, seq, ctx_len, TILE_SWA_Q)
            if need_ctx:
                a_ctx = _mla_attn(qmc, kmc, vmc, None, None, batch, ctx_len, ctx_len, seq, ctx_len)
                b_ctx = _swa_attn_ctx(swa_sink[i], qsc, ksc, vsc, batch, ctx_len)
        else:
            win = odd_w_in[i].astype(BF16)
            n_a = NA_HEADS * NA_DIM
            wo_a, wo_b = odd_w_out[i][:n_a].astype(BF16), odd_w_out[i][n_a:].astype(BF16)
            shared = (row(pre_mix_g[layer]), win, row(gqa_q_g[i]), row(gqa_k_g[i]))
            qn, kn, vn, qq, kq, vq = _pre_odd(pre_rows, x, mod_lat, *shared, tab_odd)
            qnc, knc, vnc, qqc, kqc, vqc = _pre_odd(ctx_rows, ctx, mod_ctx, *shared, tab_odd_ctx)
            na_tab = _na_bias_table(na_rpb[i], seq // GRID_W)
            a_lat = _na_attn(na_tab, qn, kn, vn, knc, vnc, batch, seq, ctx_len)
            b_lat = _gqa_attn(qq, kqc, vqc, kq, vq, batch, seq, ctx_len, TILE_GQA_Q)
            assert not need_ctx, "context-query attention for odd layers is only needed when depth > 2"
        wg, wv, cw, wd = _ffn_weights(ffn_up[layer], ffn_conv_w[layer], ffn_conv_b[layer], ffn_down[layer])
        g1, g2, g3 = row(post_mix_g[layer]), row(pre_ffn_g[layer]), row(post_ffn_g[layer])
        x = _mix_ffn(lat_rows, a_lat, b_lat, x, mod_lat, wo_a, wo_b, g1, g2, g3, wg, wv, cw, wd)
        if need_ctx:
            ctx = _mix_ffn(ctx_rows, a_ctx, b_ctx, ctx, mod_ctx, wo_a, wo_b, g1, g2, g3, wg, wv, cw, wd)
    return x.reshape(batch, seq, D_MODEL)
```

```python
import functools
import math

import jax
import jax.numpy as jnp
import numpy as np
from jax import lax
from jax.experimental import pallas as pl
from jax.experimental.pallas import tpu as pltpu

F32 = jnp.float32
BF16 = jnp.bfloat16

D_MODEL = 1024
GRID_W = 64
ROPE_BASE = 10000.0
EPS = 1e-6
NEG_INF = -1e30
LOG2E = math.log2(math.e)
KEY_CHUNK = 512
LANES = 128

MLA_HEADS, MLA_Q_RANK, MLA_KV_RANK, MLA_NOPE, MLA_ROPE, MLA_V = 8, 256, 128, 64, 32, 64
SWA_HEADS, SWA_KV_HEADS, SWA_DIM, SWA_WINDOW = 8, 2, 64, 128
NA_HEADS, NA_DIM, NA_KH, NA_KW = 8, 64, 8, 16
GQA_HEADS, GQA_KV_HEADS, GQA_DIM = 4, 2, 128
D_FF = 2816
FF_CHUNK = 256
N_FF_CHUNKS = D_FF // FF_CHUNK
HALO = 16

NA_TILE_ROWS = 4
NA_WIN_ROWS = NA_TILE_ROWS + NA_KH - 1

VMEM_LIMIT = 56 * 1024 * 1024

TILE_MIX_FFN = 512
TILE_MLA_Q = 1024
TILE_GQA_Q = 1024
TILE_SWA_Q = 512


def _params():
    return pltpu.CompilerParams(vmem_limit_bytes=VMEM_LIMIT)


def _dot(a, b):
    return jnp.dot(a, b, preferred_element_type=F32)


def _dot_nt(a, b):
    return lax.dot_general(a, b, (((1,), (1,)), ((), ())), preferred_element_type=F32)


def _rms(x, g):
    return x * lax.rsqrt(jnp.mean(x * x, axis=-1, keepdims=True) + EPS) * g


def _rope(x, c, sa, sb, sha, shb):
    return x * c + pltpu.roll(x, sha, axis=1) * sa + pltpu.roll(x, shb, axis=1) * sb


def _lane_lo(shape):
    return lax.broadcasted_iota(jnp.int32, shape, len(shape) - 1) < (LANES // 2)


def _const_spec(shape):
    nd = len(shape)
    return pl.BlockSpec(shape, lambda *_: (0,) * nd, pipeline_mode=pl.Buffered(1))


def _ada_kernel(cc_ref, w_ref, b_ref, o_ref):
    cc = cc_ref[...]
    s = cc * (1.0 / (1.0 + jnp.exp(-cc)))
    o_ref[0] = _dot(s.astype(BF16), w_ref[0].astype(BF16)) + b_ref[0]


def _ada(cc, ada_w, ada_b):
    depth = ada_w.shape[0]
    rows = cc.shape[0]
    return pl.pallas_call(
        _ada_kernel,
        grid=(depth, 6),
        in_specs=[
            pl.BlockSpec((rows, D_MODEL), lambda l, j: (0, 0)),
            pl.BlockSpec((1, D_MODEL, D_MODEL), lambda l, j: (l, 0, j)),
            pl.BlockSpec((1, 1, D_MODEL), lambda l, j: (l, 0, j)),
        ],
        out_specs=pl.BlockSpec((1, rows, D_MODEL), lambda l, j: (l, 0, j)),
        out_shape=jax.ShapeDtypeStruct((depth, rows, 6 * D_MODEL), F32),
        compiler_params=_params(),
        name="ada",
    )(cc, ada_w, ada_b.reshape(depth, 1, 6 * D_MODEL))


class _Rows:
    def __init__(self, n, seq, tm, per_batch_mod):
        assert seq % tm == 0 and n % seq == 0
        self.n, self.seq, self.tm = n, seq, tm
        self.tps = seq // tm
        self.grid = (n // tm,)
        self.per_batch_mod = per_batch_mod

    def row_spec(self, width):
        return pl.BlockSpec((self.tm, width), lambda i: (i, 0))

    def mod_spec(self):
        if self.per_batch_mod:
            tps = self.tps
            return pl.BlockSpec((1, 8, D_MODEL), lambda i: (i // tps, 0, 0))
        return pl.BlockSpec((1, 8, D_MODEL), lambda i: (0, 0, 0))

    def table_spec(self, width):
        tps = self.tps
        return pl.BlockSpec((self.tm, width), lambda i: (i % tps, 0))


PRE_HALF = 512


def _row_halves(tm):
    half = min(tm, PRE_HALF)
    assert tm % half == 0
    return [slice(r, r + half) for r in range(0, tm, half)]


def _modulated(x_ref, mod_ref, g_ref, rows, shift_row, scale_row):
    x = x_ref[rows, :]
    shift = mod_ref[0, shift_row:shift_row + 1, :]
    scale = mod_ref[0, scale_row:scale_row + 1, :]
    return _rms(x, g_ref[...]) * (1.0 + scale) + shift


EVEN_COLS = 1280


def _pre_even_kernel(x_ref, mod_ref, g_ref, win_ref, qg_ref, wuq_ref, kvg_ref, wuk_ref, wuv_ref,
                     tab_ref, qm_ref, km_ref, vm_ref, qs_ref, ks_ref, vs_ref):
    mla_scale = LOG2E * (MLA_NOPE + MLA_ROPE) ** -0.5
    swa_scale = LOG2E * SWA_DIM ** -0.5
    for rows in _row_halves(x_ref.shape[0]):
        h = _modulated(x_ref, mod_ref, g_ref, rows, 0, 1).astype(BF16)
        mc, ma, mb = tab_ref[rows, 0:128], tab_ref[rows, 128:256], tab_ref[rows, 256:384]
        sc, sa, sb = tab_ref[rows, 384:512], tab_ref[rows, 512:640], tab_ref[rows, 640:768]

        pa = _dot(h, win_ref[:, 0:512])
        cq = _rms(pa[:, 0:256], qg_ref[...]).astype(BF16)
        ckv = _rms(pa[:, 256:384], kvg_ref[...]).astype(BF16)
        kpe = _rope(pa[:, 384:512], mc, ma, mb, 112, 16)
        q = _dot(cq, wuq_ref[...])
        kn = _dot(ckv, wuk_ref[...])
        for j in range(MLA_HEADS):
            sl = slice(j * LANES, (j + 1) * LANES)
            qm_ref[rows, sl] = (_rope(q[:, sl], mc, ma, mb, 112, 16) * mla_scale).astype(BF16)
            km_ref[rows, sl] = (kn[:, sl] + kpe).astype(BF16)
        vm_ref[rows, :] = _dot(ckv, wuv_ref[...]).astype(BF16)

        pq = _dot(h, win_ref[:, 512:1024])
        for j in range(4):
            sl = slice(j * LANES, (j + 1) * LANES)
            qs_ref[rows, sl] = (_rope(pq[:, sl], sc, sa, sb, 96, 32) * swa_scale).astype(BF16)
        pkv = _dot(h, win_ref[:, 1024:1280])
        ks_ref[rows, :] = _rope(pkv[:, 0:128], sc, sa, sb, 96, 32).astype(BF16)
        vs_ref[rows, :] = pkv[:, 128:256].astype(BF16)


def _pre_even(rows, x, mod, g, win, qg, wuq, kvg, wuk, wuv, tab):
    n = rows.n
    widths = (1024, 1024, 512, 512, 128, 128)
    return pl.pallas_call(
        _pre_even_kernel,
        grid=rows.grid,
        in_specs=[
            rows.row_spec(D_MODEL), rows.mod_spec(), _const_spec((1, D_MODEL)),
            _const_spec(win.shape), _const_spec(qg.shape), _const_spec(wuq.shape),
            _const_spec(kvg.shape), _const_spec(wuk.shape), _const_spec(wuv.shape),
            rows.table_spec(768),
        ],
        out_specs=[rows.row_spec(w) for w in widths],
        out_shape=[jax.ShapeDtypeStruct((n, w), BF16) for w in widths],
        compiler_params=_params(),
        name="pre_even",
    )(x, mod, g, win, qg, wuq, kvg, wuk, wuv, tab)


def _pre_odd_kernel(x_ref, mod_ref, g_ref, win_ref, qg_ref, kg_ref, tab_ref,
                    qn_ref, kn_ref, vn_ref, qq_ref, kq_ref, vq_ref):
    na_scale = LOG2E * NA_DIM ** -0.5
    gqa_scale = LOG2E * GQA_DIM ** -0.5
    for rows in _row_halves(x_ref.shape[0]):
        h = _modulated(x_ref, mod_ref, g_ref, rows, 0, 1).astype(BF16)
        gc, gs = tab_ref[rows, 0:128], tab_ref[rows, 128:256]

        def normed(blk, gain, gc=gc, gs=gs):
            y = _rms(blk, gain)
            return y * gc + pltpu.roll(y, 64, axis=1) * gs

        qn_ref[rows, :] = (_dot(h, win_ref[:, 0:512]) * na_scale).astype(BF16)
        kn_ref[rows, :] = _dot(h, win_ref[:, 512:1024]).astype(BF16)
        vn_ref[rows, :] = _dot(h, win_ref[:, 1024:1536]).astype(BF16)
        pq = _dot(h, win_ref[:, 1536:2048])
        for j in range(GQA_HEADS):
            sl = slice(j * LANES, (j + 1) * LANES)
            qq_ref[rows, sl] = (normed(pq[:, sl], qg_ref[...]) * gqa_scale).astype(BF16)
        pkv = _dot(h, win_ref[:, 2048:2560])
        for j in range(GQA_KV_HEADS):
            sl = slice(j * LANES, (j + 1) * LANES)
            kq_ref[rows, sl] = normed(pkv[:, sl], kg_ref[...]).astype(BF16)
        vq_ref[rows, :] = pkv[:, 256:512].astype(BF16)


def _pre_odd(rows, x, mod, g, win, qg, kg, tab):
    n = rows.n
    widths = (512, 512, 512, 512, 256, 256)
    return pl.pallas_call(
        _pre_odd_kernel,
        grid=rows.grid,
        in_specs=[
            rows.row_spec(D_MODEL), rows.mod_spec(), _const_spec((1, D_MODEL)),
            _const_spec(win.shape), _const_spec(qg.shape), _const_spec(kg.shape),
            rows.table_spec(256),
        ],
        out_specs=[rows.row_spec(w) for w in widths],
        out_shape=[jax.ShapeDtypeStruct((n, w), BF16) for w in widths],
        compiler_params=_params(),
        name="pre_odd",
    )(x, mod, g, win, qg, kg, tab)


def _attend(q, chunks, sink=None):
    return _attend_many([(q, chunks, sink)])[0]


def _attend_many(streams):
    state = []
    for q, _, sink in streams:
        if sink is None:
            state.append((None, None))
        else:
            rows = q.shape[0]
            state.append((sink, jnp.concatenate([jnp.zeros((rows, LANES), F32), jnp.ones((rows, LANES), F32)],
                                                axis=1)))
    for c in range(max(len(chunks) for _, chunks, _ in streams)):
        for idx, (q, chunks, _) in enumerate(streams):
            if c >= len(chunks):
                continue
            k, v, bias = chunks[c]
            m, acc = state[idx]
            s = _dot_nt(q, k)
            if bias is not None:
                s = s + bias
            cm = jnp.max(s, axis=-1, keepdims=True)
            m_new = cm if m is None else jnp.maximum(m, cm)
            p = jnp.exp2(s - m_new).astype(BF16)
            vx = jnp.concatenate([v, jnp.ones(v.shape, BF16)], axis=1)
            pv = _dot(p, vx)
            acc = pv if acc is None else jnp.exp2(m - m_new) * acc + pv
            state[idx] = (m_new, acc)
    return [acc[:, :LANES] / acc[:, LANES:] for _, acc in state]


def _chunks(k_ref, v_ref, lanes, size):
    n = k_ref.shape[0]
    assert n % size == 0
    return [(k_ref[c * size:(c + 1) * size, lanes], v_ref[c * size:(c + 1) * size, :], None)
            for c in range(n // size)]


def _mla_kernel(has_lat, q_ref, kc_ref, vc_ref, *rest):
    if has_lat:
        kl_ref, vl_ref, o_ref = rest
    else:
        (o_ref,) = rest
    streams = []
    n_pairs = o_ref.shape[1] // LANES
    for e in range(2 * n_pairs):
        sl = slice(e * LANES, (e + 1) * LANES)
        vl_lanes = slice((e // 2) * LANES, (e // 2 + 1) * LANES)
        chunks = [(kc_ref[:, sl], vc_ref[:, vl_lanes], None)]
        if has_lat:
            chunks += [(k, v[:, vl_lanes], b) for k, v, b in _chunks(kl_ref, vl_ref, sl, KEY_CHUNK)]
        streams.append((q_ref[:, sl], chunks, None))
    outs = _attend_many(streams)
    lo = _lane_lo(outs[0].shape)
    for j in range(n_pairs):
        o_ref[:, j * LANES:(j + 1) * LANES] = jnp.where(lo, outs[2 * j], outs[2 * j + 1]).astype(BF16)


def _mla_attn(q, kc, vc, kl, vl, batch, q_seq, ctx_len, lat_seq, tq):
    has_lat = kl is not None
    nq = q_seq // tq
    pairs = 1 if has_lat else MLA_HEADS // 2
    in_specs = [
        pl.BlockSpec((tq, 256 * pairs), lambda b, j, t: (b * nq + t, j)),
        pl.BlockSpec((ctx_len, 256 * pairs), lambda b, j, t: (b, j)),
        pl.BlockSpec((ctx_len, 128 * pairs), lambda b, j, t: (b, j)),
    ]
    args = [q, kc, vc]
    if has_lat:
        in_specs += [
            pl.BlockSpec((lat_seq, 256), lambda b, j, t: (b, j)),
            pl.BlockSpec((lat_seq, 128), lambda b, j, t: (b, j)),
        ]
        args += [kl, vl]
    return pl.pallas_call(
        functools.partial(_mla_kernel, has_lat),
        grid=(batch, MLA_HEADS // 2 // pairs, nq),
        in_specs=in_specs,
        out_specs=pl.BlockSpec((tq, 128 * pairs), lambda b, j, t: (b * nq + t, j)),
        out_shape=jax.ShapeDtypeStruct((batch * q_seq, MLA_HEADS * MLA_V), BF16),
        compiler_params=_params(),
        name="mla_attn_lat" if has_lat else "mla_attn_ctx",
    )(*args)


def _gqa_kernel(q_ref, kc_ref, vc_ref, kl_ref, vl_ref, o_ref):
    tq = q_ref.shape[0]
    per_group = GQA_HEADS // GQA_KV_HEADS
    q = jnp.concatenate([q_ref[:, e * LANES:(e + 1) * LANES] for e in range(per_group)], axis=0)
    every = slice(None)
    chunks = _chunks(kc_ref, vc_ref, every, kc_ref.shape[0]) + _chunks(kl_ref, vl_ref, every, KEY_CHUNK)
    o = _attend(q, chunks).astype(BF16)
    for e in range(per_group):
        o_ref[:, e * LANES:(e + 1) * LANES] = o[e * tq:(e + 1) * tq]


def _gqa_attn(q, kc, vc, kl, vl, batch, seq, ctx_len, tq):
    nq = seq // tq
    return pl.pallas_call(
        _gqa_kernel,
        grid=(batch, GQA_KV_HEADS, nq),
        in_specs=[
            pl.BlockSpec((tq, 256), lambda b, g, t: (b * nq + t, g)),
            pl.BlockSpec((ctx_len, 128), lambda b, g, t: (b, g)),
            pl.BlockSpec((ctx_len, 128), lambda b, g, t: (b, g)),
            pl.BlockSpec((seq, 128), lambda b, g, t: (b, g)),
            pl.BlockSpec((seq, 128), lambda b, g, t: (b, g)),
        ],
        out_specs=pl.BlockSpec((tq, 256), lambda b, g, t: (b * nq + t, g)),
        out_shape=jax.ShapeDtypeStruct((batch * seq, GQA_HEADS * GQA_DIM), BF16),
        compiler_params=_params(),
        name="gqa_attn",
    )(q, kc, vc, kl, vl)


SWA_QB = 128
SWA_GROUP = SWA_HEADS // SWA_KV_HEADS


def _swa_group_lhs(q_ref, r0, rows, g):
    lo = _lane_lo((rows, LANES))
    keep = lo if g == 0 else jnp.logical_not(lo)
    blocks = [jnp.where(keep, q_ref[r0:r0 + rows, j * LANES:(j + 1) * LANES], jnp.zeros((), BF16))
              for j in range(SWA_GROUP)]
    return jnp.concatenate(blocks, axis=0)


def _swa_sink_col(sink_ref, g, rows):
    blk = lax.broadcasted_iota(jnp.int32, (SWA_GROUP * rows, 1), 0) // rows
    col = jnp.full((SWA_GROUP * rows, 1), sink_ref[g * SWA_GROUP], F32)
    for j in range(1, SWA_GROUP):
        col = jnp.where(blk == j, sink_ref[g * SWA_GROUP + j], col)
    return col * LOG2E


def _swa_store(o_ref, r0, rows, o_groups):
    lo = _lane_lo((rows, LANES))
    for j in range(SWA_GROUP):
        a = o_groups[0][j * rows:(j + 1) * rows]
        b = o_groups[1][j * rows:(j + 1) * rows]
        o_ref[r0:r0 + rows, j * LANES:(j + 1) * LANES] = jnp.where(lo, a, b).astype(BF16)


SWA_BAND = SWA_QB + 2 * SWA_WINDOW


def _swa_mask_table():
    ii = np.arange(SWA_GROUP * SWA_QB)[:, None] % SWA_QB
    jj = np.arange(SWA_BAND)[None, :]
    in_window = np.abs(jj - SWA_WINDOW - ii) <= SWA_WINDOW
    first = in_window & (jj >= SWA_WINDOW)
    last = in_window & (jj < SWA_WINDOW + SWA_QB)
    return jnp.asarray(np.where(np.stack([first, in_window, last]), 0.0, NEG_INF), F32)


def _swa_lat_kernel(tq, sink_ref, mask_ref, q_ref, kp_ref, kt_ref, kn_ref, vp_ref, vt_ref, vn_ref,
                    kc_ref, vc_ref, o_ref):
    t = pl.program_id(1)
    kband = jnp.concatenate([kp_ref[...], kt_ref[...], kn_ref[...]], axis=0)
    vband = jnp.concatenate([vp_ref[...], vt_ref[...], vn_ref[...]], axis=0)
    band = SWA_BAND
    n_blocks = tq // SWA_QB
    for i in range(n_blocks):
        variant = 1
        if i == 0:
            variant = jnp.where(t == 0, 0, variant)
        if i == n_blocks - 1:
            variant = jnp.where(t == pl.num_programs(1) - 1, 2, variant)
        bias = mask_ref[variant]
        kb = kband[i * SWA_QB:i * SWA_QB + band]
        vb = vband[i * SWA_QB:i * SWA_QB + band]
        o_groups = []
        for g in range(SWA_KV_HEADS):
            lhs = _swa_group_lhs(q_ref, i * SWA_QB, SWA_QB, g)
            o_groups.append(_attend(lhs, [(kc_ref[...], vc_ref[...], None), (kb, vb, bias)],
                                    _swa_sink_col(sink_ref, g, SWA_QB)))
        _swa_store(o_ref, i * SWA_QB, SWA_QB, o_groups)


def _swa_ctx_kernel(ctx_len, sink_ref, q_ref, kc_ref, vc_ref, o_ref):
    o_groups = []
    for g in range(SWA_KV_HEADS):
        lhs = _swa_group_lhs(q_ref, 0, ctx_len, g)
        o_groups.append(_attend(lhs, [(kc_ref[...], vc_ref[...], None)], _swa_sink_col(sink_ref, g, ctx_len)))
    _swa_store(o_ref, 0, ctx_len, o_groups)


def _swa_attn_lat(sink, q, k, v, kc, vc, batch, seq, ctx_len, tq):
    nq = seq // tq
    nb = seq // SWA_WINDOW
    per = tq // SWA_WINDOW
    prev_spec = pl.BlockSpec((SWA_WINDOW, 128), lambda b, t: (b * nb + jnp.maximum(t * per - 1, 0), 0))
    next_spec = pl.BlockSpec((SWA_WINDOW, 128), lambda b, t: (b * nb + jnp.minimum(t * per + per, nb - 1), 0))
    tile_spec = pl.BlockSpec((tq, 128), lambda b, t: (b * nq + t, 0))
    ctx_spec = pl.BlockSpec((ctx_len, 128), lambda b, t: (b, 0))
    mask = _swa_mask_table()
    return pl.pallas_call(
        functools.partial(_swa_lat_kernel, tq),
        grid=(batch, nq),
        in_specs=[
            pl.BlockSpec(memory_space=pltpu.SMEM),
            pl.BlockSpec(mask.shape, lambda b, t: (0, 0, 0)),
            pl.BlockSpec((tq, 512), lambda b, t: (b * nq + t, 0)),
            prev_spec, tile_spec, next_spec, prev_spec, tile_spec, next_spec,
            ctx_spec, ctx_spec,
        ],
        out_specs=pl.BlockSpec((tq, 512), lambda b, t: (b * nq + t, 0)),
        out_shape=jax.ShapeDtypeStruct((batch * seq, SWA_HEADS * SWA_DIM), BF16),
        compiler_params=_params(),
        name="swa_attn_lat",
    )(sink, mask, q, k, k, k, v, v, v, kc, vc)


def _swa_attn_ctx(sink, q, kc, vc, batch, ctx_len):
    return pl.pallas_call(
        functools.partial(_swa_ctx_kernel, ctx_len),
        grid=(batch,),
        in_specs=[
            pl.BlockSpec(memory_space=pltpu.SMEM),
            pl.BlockSpec((ctx_len, 512), lambda b: (b, 0)),
            pl.BlockSpec((ctx_len, 128), lambda b: (b, 0)),
            pl.BlockSpec((ctx_len, 128), lambda b: (b, 0)),
        ],
        out_specs=pl.BlockSpec((ctx_len, 512), lambda b: (b, 0)),
        out_shape=jax.ShapeDtypeStruct((batch * ctx_len, SWA_HEADS * SWA_DIM), BF16),
        compiler_params=_params(),
        name="swa_attn_ctx",
    )(sink, q, kc, vc)


NA_TILES_PER_STEP = 8


def _na_kernel(n_tiles, tab_ref, q_ref, k_ref, v_ref, kc_ref, vc_ref, o_ref):
    grid_rows = n_tiles * NA_TILE_ROWS
    tq = NA_TILE_ROWS * GRID_W
    lo = _lane_lo((tq, LANES))
    streams = []
    for u in range(NA_TILES_PER_STEP):
        t = pl.program_id(2) * NA_TILES_PER_STEP + u
        ws = jnp.clip(t * NA_TILE_ROWS - NA_KH // 2, 0, grid_rows - NA_WIN_ROWS)
        variant = jnp.where(t == 0, 0, jnp.where(t == n_tiles - 1, 2, 1))
        start = pl.multiple_of(ws * GRID_W, GRID_W)
        kwin = k_ref[pl.ds(start, NA_WIN_ROWS * GRID_W), :]
        vwin = v_ref[pl.ds(start, NA_WIN_ROWS * GRID_W), :]
        q = q_ref[u * tq:(u + 1) * tq, :]
        zero = jnp.zeros((), BF16)
        lhs = jnp.concatenate([jnp.where(lo, q, zero), jnp.where(lo, zero, q)], axis=0)
        bias = tab_ref[variant].reshape(2 * tq, NA_WIN_ROWS * GRID_W)
        streams.append((lhs, [(kc_ref[...], vc_ref[...], None), (kwin, vwin, bias)], None))
    for u, o in enumerate(_attend_many(streams)):
        o_ref[u * tq:(u + 1) * tq, :] = jnp.where(lo, o[:tq], o[tq:]).astype(BF16)


def _na_attn(tab, q, k, v, kc, vc, batch, seq, ctx_len):
    tq = NA_TILE_ROWS * GRID_W
    n_tiles = seq // tq
    nk = NA_WIN_ROWS * GRID_W
    assert n_tiles % NA_TILES_PER_STEP == 0
    n_steps = n_tiles // NA_TILES_PER_STEP
    tq_step = tq * NA_TILES_PER_STEP
    return pl.pallas_call(
        functools.partial(_na_kernel, n_tiles),
        grid=(NA_HEADS // 2, batch, n_steps),
        in_specs=[
            pl.BlockSpec((3, 2, tq, nk), lambda j, b, t: (0, j, 0, 0)),
            pl.BlockSpec((tq_step, 128), lambda j, b, t: (b * n_steps + t, j)),
            pl.BlockSpec((seq, 128), lambda j, b, t: (b, j)),
            pl.BlockSpec((seq, 128), lambda j, b, t: (b, j)),
            pl.BlockSpec((ctx_len, 128), lambda j, b, t: (b, j)),
            pl.BlockSpec((ctx_len, 128), lambda j, b, t: (b, j)),
        ],
        out_specs=pl.BlockSpec((tq_step, 128), lambda j, b, t: (b * n_steps + t, j)),
        out_shape=jax.ShapeDtypeStruct((batch * seq, NA_HEADS * NA_DIM), BF16),
        compiler_params=_params(),
        name="na_attn",
    )(tab, q, k, v, kc, vc)


def _na_bias_table(rpb, grid_rows):
    cols = np.arange(GRID_W)
    col_start = np.clip(cols - NA_KW // 2, 0, GRID_W - NA_KW)
    col_ok = (cols[None, :] >= col_start[:, None]) & (cols[None, :] < col_start[:, None] + NA_KW)
    dc = np.clip(cols[None, :] - cols[:, None], -(NA_KW - 1), NA_KW - 1) + NA_KW - 1
    n_heads, n_dr = rpb.shape[0], rpb.shape[1]
    by_col = jnp.take(rpb.astype(F32) * LOG2E, jnp.asarray(dc.reshape(-1)), axis=2).reshape(n_heads, n_dr, GRID_W, GRID_W)
    by_col = jnp.where(jnp.asarray(col_ok)[None, None], by_col, NEG_INF)
    by_col = jnp.concatenate([by_col, jnp.full((n_heads, 1, GRID_W, GRID_W), NEG_INF, F32)], axis=1)
    n_tiles = grid_rows // NA_TILE_ROWS
    variants = []
    for t in (0, min(2, n_tiles - 1), n_tiles - 1):
        r0 = t * NA_TILE_ROWS
        ws = int(np.clip(r0 - NA_KH // 2, 0, grid_rows - NA_WIN_ROWS))
        r = r0 + np.arange(NA_TILE_ROWS)
        rs = np.clip(r - NA_KH // 2, 0, grid_rows - NA_KH)
        key_row = ws + np.arange(NA_WIN_ROWS)
        row_ok = (key_row[None, :] >= rs[:, None]) & (key_row[None, :] < rs[:, None] + NA_KH)
        dr = np.where(row_ok, key_row[None, :] - r[:, None] + NA_KH - 1, n_dr)
        rows = []
        for i in range(NA_TILE_ROWS):
            rows.append(jnp.concatenate([by_col[:, int(d)] for d in dr[i]], axis=-1))
        variants.append(jnp.concatenate(rows, axis=1))
    return jnp.stack(variants, axis=0)


def _gelu_tanh(x):
    return x * (0.5 * (1.0 + jnp.tanh(math.sqrt(2.0 / math.pi) * (x + 0.044715 * (x * x * x)))))


def _mix_ffn_kernel(tm, tps, a_ref, b_ref, x_ref, modp_ref, modf_ref, wa_ref, wb_ref, g1_ref, g2_ref, g3_ref,
                    wg_ref, wv_ref, cw_ref, wd_ref, o_ref, hh_ref, x1_ref, ug_ref, uv_ref, act_ref, acc_ref):
    i = pl.program_id(0)
    zero = jnp.zeros((), BF16)

    @pl.when(i == 0)
    def _():
        hh_ref[...] = jnp.zeros_like(hh_ref)
        x1_ref[...] = jnp.zeros_like(x1_ref)

    def step(s):
        o = 1 - s
        y = _dot(a_ref[...], wa_ref[...]) + _dot(b_ref[...], wb_ref[...])
        x1 = x_ref[...] + modp_ref[0, 2:3, :] * _rms(y, g1_ref[...])
        h = (_rms(x1, g2_ref[...]) * (1.0 + modp_ref[0, 4:5, :]) + modp_ref[0, 3:4, :]).astype(BF16)
        x1_ref[s] = x1
        hh_ref[s, HALO:HALO + tm] = h
        first = i % tps == 0
        hh_ref[s, 0:HALO] = jnp.where(first, zero, hh_ref[o, tm:tm + HALO])
        hh_ref[o, HALO + tm:] = jnp.where(first, zero, h[0:HALO])

        def up(c, slot):
            uv_ref[slot] = _dot(hh_ref[o, HALO:HALO + tm], wv_ref[c])
            ug_ref[slot] = _dot(hh_ref[o], wg_ref[c])

        def activate(c, slot):
            cw = cw_ref[c]
            gate = cw[3:4]
            for tap in range(3):
                gate = gate + ug_ref[slot, HALO - 1 + tap:HALO - 1 + tap + tm] * cw[tap:tap + 1]
            act_ref[c] = (_gelu_tanh(gate) * uv_ref[slot]).astype(BF16)

        up(0, 0)
        for c in range(N_FF_CHUNKS):
            if c + 1 < N_FF_CHUNKS:
                up(c + 1, (c + 1) % 2)
            activate(c, c % 2)

        for nb in range(D_MODEL // FF_CHUNK):
            cols = slice(nb * FF_CHUNK, (nb + 1) * FF_CHUNK)
            f = _dot(act_ref[0], wd_ref[0, :, cols])
            for c in range(1, N_FF_CHUNKS):
                f = f + _dot(act_ref[c], wd_ref[c, :, cols])
            acc_ref[:, cols] = f
        o_ref[...] = x1_ref[o] + modf_ref[0, 5:6, :] * _rms(acc_ref[...], g3_ref[...])

    pl.when(i % 2 == 0)(lambda: step(0))
    pl.when(i % 2 == 1)(lambda: step(1))


def _mix_ffn(rows, a, b, x, mod, wa, wb, g1, g2, g3, wg, wv, cw, wd):
    n, tm, tps = rows.n, rows.tm, rows.tps
    n_tiles = n // tm
    assert n_tiles % tps == 0

    def tile_in(width):
        return pl.BlockSpec((tm, width), lambda i: (jnp.minimum(i, n_tiles - 1), 0))

    def mod_spec(tile_of_step):
        if rows.per_batch_mod:
            return pl.BlockSpec((1, 8, D_MODEL), lambda i: (tile_of_step(i) // tps, 0, 0))
        return pl.BlockSpec((1, 8, D_MODEL), lambda i: (0, 0, 0))

    row = _const_spec((1, D_MODEL))
    return pl.pallas_call(
        functools.partial(_mix_ffn_kernel, tm, tps),
        grid=(n_tiles + 1,),
        in_specs=[
            tile_in(a.shape[1]), tile_in(b.shape[1]), tile_in(D_MODEL),
            mod_spec(lambda i: jnp.minimum(i, n_tiles - 1)), mod_spec(lambda i: jnp.maximum(i - 1, 0)),
            _const_spec(wa.shape), _const_spec(wb.shape), row, row, row,
            _const_spec(wg.shape), _const_spec(wv.shape), _const_spec(cw.shape), _const_spec(wd.shape),
        ],
        out_specs=pl.BlockSpec((tm, D_MODEL), lambda i: (jnp.maximum(i - 1, 0), 0)),
        out_shape=jax.ShapeDtypeStruct((n, D_MODEL), F32),
        scratch_shapes=[
            pltpu.VMEM((2, tm + 2 * HALO, D_MODEL), BF16),
            pltpu.VMEM((2, tm, D_MODEL), F32),
            pltpu.VMEM((2, tm + 2 * HALO, FF_CHUNK), F32),
            pltpu.VMEM((2, tm, FF_CHUNK), F32),
            pltpu.VMEM((N_FF_CHUNKS, tm, FF_CHUNK), BF16),
            pltpu.VMEM((tm, D_MODEL), F32),
        ],
        compiler_params=_params(),
        name="mix_ffn",
    )(a, b, x, mod, mod, wa, wb, g1, g2, g3, wg, wv, cw, wd)


def _rope_angles(n, d):
    t = jnp.arange(n)
    rows = (t // GRID_W).astype(F32)
    cols = (t % GRID_W).astype(F32)
    freqs = ROPE_BASE ** (-jnp.arange(d // 4, dtype=F32) * 4.0 / d)
    ang = jnp.concatenate([rows[:, None] * freqs, cols[:, None] * freqs], axis=-1)
    return jnp.cos(ang), jnp.sin(ang)


def _rope_lane_tables(cos, sin, offset, group):
    n, half = cos.shape
    c = jnp.ones((n, group), F32)
    sa = jnp.zeros((n, group), F32)
    sb = jnp.zeros((n, group), F32)
    c = c.at[:, offset:offset + half].set(cos).at[:, offset + half:offset + 2 * half].set(cos)
    sa = sa.at[:, offset:offset + half].set(-sin)
    sb = sb.at[:, offset + half:offset + 2 * half].set(sin)
    reps = LANES // group
    return tuple(jnp.tile(a, (1, reps)) for a in (c, sa, sb))


def _identity_tables(n, width_blocks):
    ones = jnp.ones((n, LANES), F32)
    zeros = jnp.zeros((n, LANES), F32)
    return [ones, zeros, zeros][:width_blocks]


def _even_weights(w_in, w_out, w_uq, w_ukv):
    z = lambda r, c: jnp.zeros((r, c), F32)
    mla_in = MLA_Q_RANK + MLA_KV_RANK + MLA_ROPE
    kpe = w_in[:, MLA_Q_RANK + MLA_KV_RANK:mla_in]
    swa = w_in[:, mla_in:]
    swa_q = swa[:, :SWA_HEADS * SWA_DIM].reshape(D_MODEL, SWA_HEADS, SWA_DIM)
    pair_order = np.array([h for j in range(SWA_GROUP) for h in (j, SWA_GROUP + j)])
    swa_q = swa_q[:, pair_order, :].reshape(D_MODEL, SWA_HEADS * SWA_DIM)
    win = jnp.concatenate([
        w_in[:, :MLA_Q_RANK + MLA_KV_RANK], z(D_MODEL, 64), kpe, z(D_MODEL, 32), swa_q,
        swa[:, SWA_HEADS * SWA_DIM:]], axis=1)
    assert win.shape[1] == EVEN_COLS
    wuq = w_uq.reshape(MLA_Q_RANK, MLA_HEADS, MLA_NOPE + MLA_ROPE)
    wuq = jnp.pad(wuq, ((0, 0), (0, 0), (0, LANES - MLA_NOPE - MLA_ROPE))).reshape(MLA_Q_RANK, MLA_HEADS * LANES)
    wukv = w_ukv.reshape(MLA_KV_RANK, MLA_HEADS, MLA_NOPE + MLA_V)
    wuk = jnp.pad(wukv[:, :, :MLA_NOPE], ((0, 0), (0, 0), (0, LANES - MLA_NOPE))).reshape(MLA_KV_RANK, MLA_HEADS * LANES)
    wuv = wukv[:, :, MLA_NOPE:].reshape(MLA_KV_RANK, MLA_HEADS * MLA_V)
    n_a = MLA_HEADS * MLA_V
    wo_a = w_out[:n_a]
    wo_b = w_out[n_a:].reshape(SWA_HEADS, SWA_DIM, D_MODEL)[pair_order].reshape(SWA_HEADS * SWA_DIM, D_MODEL)
    return tuple(a.astype(BF16) for a in (win, wuq, wuk, wuv, wo_a, wo_b))


def _ffn_weights(w_up, conv_w, conv_b, w_down):
    def chunks(w):
        return w.reshape(D_MODEL, N_FF_CHUNKS, FF_CHUNK).transpose(1, 0, 2).astype(BF16)

    cw = jnp.concatenate([conv_w, conv_b[None], jnp.zeros((4, D_FF), F32)], axis=0)
    cw = cw.reshape(8, N_FF_CHUNKS, FF_CHUNK).transpose(1, 0, 2)
    wd = w_down.reshape(N_FF_CHUNKS, FF_CHUNK, D_MODEL).astype(BF16)
    return chunks(w_up[:, :D_FF]), chunks(w_up[:, D_FF:]), cw, wd


def kernel(x, c, ctx, c_ctx, ada_w, ada_b, pre_mix_g, post_mix_g, pre_ffn_g, post_ffn_g, ffn_up, ffn_conv_w,
           ffn_conv_b, ffn_down, even_w_in, even_w_out, mla_q_norm_g, mla_w_uq, mla_kv_norm_g, mla_w_ukv,
           swa_sink, odd_w_in, odd_w_out, na_rpb, gqa_q_g, gqa_k_g):
    batch, seq, _ = x.shape
    ctx_len = ctx.shape[1]
    depth = ada_w.shape[0]
    n_lat, n_ctx = batch * seq, batch * ctx_len

    pad_rows = -(batch + 1) % 8
    cc = jnp.concatenate([c, c_ctx[None], jnp.zeros((pad_rows, D_MODEL), F32)], axis=0)
    mods = _ada(cc, ada_w, ada_b).reshape(depth, cc.shape[0], 6, D_MODEL)
    mods = jnp.pad(mods, ((0, 0), (0, 0), (0, 2), (0, 0)))

    lat_rows = _Rows(n_lat, seq, TILE_MIX_FFN, True)
    pre_rows = _Rows(n_lat, seq, 2 * PRE_HALF, True)
    ctx_rows = _Rows(n_ctx, ctx_len, ctx_len, False)
    pre_ctx_tile = 2 * PRE_HALF if n_ctx % (2 * PRE_HALF) == 0 else ctx_len
    pre_ctx_rows = _Rows(n_ctx, pre_ctx_tile, pre_ctx_tile, False)

    cos_m, sin_m = _rope_angles(seq, MLA_ROPE)
    cos_s, sin_s = _rope_angles(seq, SWA_DIM)
    cos_g, sin_g = _rope_angles(seq, GQA_DIM)
    gc, gsa, gsb = _rope_lane_tables(cos_g, sin_g, 0, LANES)
    tab_even = jnp.concatenate(_rope_lane_tables(cos_m, sin_m, MLA_NOPE, LANES)
                               + _rope_lane_tables(cos_s, sin_s, 0, SWA_DIM), axis=1)
    tab_odd = jnp.concatenate([gc, gsa + gsb], axis=1)
    tab_even_ctx = jnp.concatenate(_identity_tables(pre_ctx_tile, 3) * 2, axis=1)
    tab_odd_ctx = jnp.concatenate(_identity_tables(pre_ctx_tile, 2), axis=1)

    x = x.reshape(n_lat, D_MODEL)
    ctx = ctx.reshape(n_ctx, D_MODEL)
    row = lambda v: v.reshape(1, -1)

    for layer in range(depth):
        need_ctx = layer < depth - 1
        i = layer // 2
        mod_lat = mods[layer, :batch]
        mod_ctx = mods[layer, batch:batch + 1]
        if layer % 2 == 0:
            win, wuq, wuk, wuv, wo_a, wo_b = _even_weights(even_w_in[i], even_w_out[i], mla_w_uq[i], mla_w_ukv[i])
            shared = (row(pre_mix_g[layer]), win, row(mla_q_norm_g[i]), wuq, row(mla_kv_norm_g[i]), wuk, wuv)
            qm, km, vm, qs, ks, vs = _pre_even(pre_rows, x, mod_lat, *shared, tab_even)
            qmc, kmc, vmc, qsc, ksc, vsc = _pre_even(pre_ctx_rows, ctx, mod_ctx, *shared, tab_even_ctx)
            a_lat = _mla_attn(qm, kmc, vmc, km, vm, batch, seq, ctx_len, seq, TILE_MLA_Q)
            b_lat = _swa_attn_lat(swa_sink[i], qs, ks, vs, ksc, vsc, batch, seq, ctx_len, TILE_SWA_Q)
            if need_ctx:
                a_ctx = _mla_attn(qmc, kmc, vmc, None, None, batch, ctx_len, ctx_len, seq, ctx_len)
                b_ctx = _swa_attn_ctx(swa_sink[i], qsc, ksc, vsc, batch, ctx_len)
        else:
            win = odd_w_in[i].astype(BF16)
            n_a = NA_HEADS * NA_DIM
            wo_a, wo_b = odd_w_out[i][:n_a].astype(BF16), odd_w_out[i][n_a:].astype(BF16)
            shared = (row(pre_mix_g[layer]), win, row(gqa_q_g[i]), row(gqa_k_g[i]))
            qn, kn, vn, qq, kq, vq = _pre_odd(pre_rows, x, mod_lat, *shared, tab_odd)
            qnc, knc, vnc, qqc, kqc, vqc = _pre_odd(pre_ctx_rows, ctx, mod_ctx, *shared, tab_odd_ctx)
            na_tab = _na_bias_table(na_rpb[i], seq // GRID_W)
            a_lat = _na_attn(na_tab, qn, kn, vn, knc, vnc, batch, seq, ctx_len)
            b_lat = _gqa_attn(qq, kqc, vqc, kq, vq, batch, seq, ctx_len, TILE_GQA_Q)
            assert not need_ctx, "context-query attention for odd layers is only needed when depth > 2"
        wg, wv, cw, wd = _ffn_weights(ffn_up[layer], ffn_conv_w[layer], ffn_conv_b[layer], ffn_down[layer])
        g1, g2, g3 = row(post_mix_g[layer]), row(pre_ffn_g[layer]), row(post_ffn_g[layer])
        x = _mix_ffn(lat_rows, a_lat, b_lat, x, mod_lat, wo_a, wo_b, g1, g2, g3, wg, wv, cw, wd)
        if need_ctx:
            ctx = _mix_ffn(ctx_rows, a_ctx, b_ctx, ctx, mod_ctx, wo_a, wo_b, g1, g2, g3, wg, wv, cw, wd)
    return x.reshape(batch, seq, D_MODEL)
```

```python
import functools
import math

import jax
import jax.numpy as jnp
import numpy as np
from jax import lax
from jax.experimental import pallas as pl
from jax.experimental.pallas import tpu as pltpu

F32 = jnp.float32
BF16 = jnp.bfloat16

D_MODEL = 1024
GRID_W = 64
ROPE_BASE = 10000.0
EPS = 1e-6
NEG_INF = -1e30
LOG2E = math.log2(math.e)
KEY_CHUNK = 512
LANES = 128

MLA_HEADS, MLA_Q_RANK, MLA_KV_RANK, MLA_NOPE, MLA_ROPE, MLA_V = 8, 256, 128, 64, 32, 64
SWA_HEADS, SWA_KV_HEADS, SWA_DIM, SWA_WINDOW = 8, 2, 64, 128
NA_HEADS, NA_DIM, NA_KH, NA_KW = 8, 64, 8, 16
GQA_HEADS, GQA_KV_HEADS, GQA_DIM = 4, 2, 128
D_FF = 2816
FF_CHUNK = 256
N_FF_CHUNKS = D_FF // FF_CHUNK
HALO = 16

NA_TILE_ROWS = 4
NA_WIN_ROWS = NA_TILE_ROWS + NA_KH - 1

VMEM_LIMIT = 56 * 1024 * 1024

TILE_MIX_FFN = 512
TILE_MLA_Q = 1024
TILE_GQA_Q = 1024
TILE_SWA_Q = 512


def _params():
    return pltpu.CompilerParams(vmem_limit_bytes=VMEM_LIMIT)


def _dot(a, b):
    return jnp.dot(a, b, preferred_element_type=F32)


def _dot_nt(a, b):
    return lax.dot_general(a, b, (((1,), (1,)), ((), ())), preferred_element_type=F32)


def _rms(x, g):
    return x * lax.rsqrt(jnp.mean(x * x, axis=-1, keepdims=True) + EPS) * g


def _rope(x, c, sa, sb, sha, shb):
    return x * c + pltpu.roll(x, sha, axis=1) * sa + pltpu.roll(x, shb, axis=1) * sb


def _lane_lo(shape):
    return lax.broadcasted_iota(jnp.int32, shape, len(shape) - 1) < (LANES // 2)


def _const_spec(shape):
    nd = len(shape)
    return pl.BlockSpec(shape, lambda *_: (0,) * nd, pipeline_mode=pl.Buffered(1))


def _ada_kernel(cc_ref, w_ref, b_ref, o_ref):
    cc = cc_ref[...]
    s = cc * (1.0 / (1.0 + jnp.exp(-cc)))
    o_ref[0] = _dot(s.astype(BF16), w_ref[0].astype(BF16)) + b_ref[0]


def _ada(cc, ada_w, ada_b):
    depth = ada_w.shape[0]
    rows = cc.shape[0]
    return pl.pallas_call(
        _ada_kernel,
        grid=(depth, 6),
        in_specs=[
            pl.BlockSpec((rows, D_MODEL), lambda l, j: (0, 0)),
            pl.BlockSpec((1, D_MODEL, D_MODEL), lambda l, j: (l, 0, j)),
            pl.BlockSpec((1, 1, D_MODEL), lambda l, j: (l, 0, j)),
        ],
        out_specs=pl.BlockSpec((1, rows, D_MODEL), lambda l, j: (l, 0, j)),
        out_shape=jax.ShapeDtypeStruct((depth, rows, 6 * D_MODEL), F32),
        compiler_params=_params(),
        name="ada",
    )(cc, ada_w, ada_b.reshape(depth, 1, 6 * D_MODEL))


class _Rows:
    def __init__(self, n, seq, tm, per_batch_mod):
        assert seq % tm == 0 and n % seq == 0
        self.n, self.seq, self.tm = n, seq, tm
        self.tps = seq // tm
        self.grid = (n // tm,)
        self.per_batch_mod = per_batch_mod

    def row_spec(self, width):
        return pl.BlockSpec((self.tm, width), lambda i: (i, 0))

    def mod_spec(self):
        if self.per_batch_mod:
            tps = self.tps
            return pl.BlockSpec((1, 8, D_MODEL), lambda i: (i // tps, 0, 0))
        return pl.BlockSpec((1, 8, D_MODEL), lambda i: (0, 0, 0))

    def table_spec(self, width):
        tps = self.tps
        return pl.BlockSpec((self.tm, width), lambda i: (i % tps, 0))


PRE_HALF = 512


def _row_halves(tm):
    half = min(tm, PRE_HALF)
    assert tm % half == 0
    return [slice(r, r + half) for r in range(0, tm, half)]


def _modulated(x_ref, mod_ref, g_ref, rows, shift_row, scale_row):
    x = x_ref[rows, :]
    shift = mod_ref[0, shift_row:shift_row + 1, :]
    scale = mod_ref[0, scale_row:scale_row + 1, :]
    return _rms(x, g_ref[...]) * (1.0 + scale) + shift


EVEN_COLS = 1280


def _pre_even_kernel(x_ref, mod_ref, g_ref, win_ref, qg_ref, wuq_ref, kvg_ref, wuk_ref, wuv_ref,
                     tab_ref, qm_ref, km_ref, vm_ref, qs_ref, ks_ref, vs_ref):
    mla_scale = LOG2E * (MLA_NOPE + MLA_ROPE) ** -0.5
    swa_scale = LOG2E * SWA_DIM ** -0.5
    for rows in _row_halves(x_ref.shape[0]):
        h = _modulated(x_ref, mod_ref, g_ref, rows, 0, 1).astype(BF16)
        mc, ma, mb = tab_ref[rows, 0:128], tab_ref[rows, 128:256], tab_ref[rows, 256:384]
        sc, sa, sb = tab_ref[rows, 384:512], tab_ref[rows, 512:640], tab_ref[rows, 640:768]

        pa = _dot(h, win_ref[:, 0:512])
        cq = _rms(pa[:, 0:256], qg_ref[...]).astype(BF16)
        ckv = _rms(pa[:, 256:384], kvg_ref[...]).astype(BF16)
        kpe = _rope(pa[:, 384:512], mc, ma, mb, 112, 16)
        q = _dot(cq, wuq_ref[...])
        kn = _dot(ckv, wuk_ref[...])
        for j in range(MLA_HEADS):
            sl = slice(j * LANES, (j + 1) * LANES)
            qm_ref[rows, sl] = (_rope(q[:, sl], mc, ma, mb, 112, 16) * mla_scale).astype(BF16)
            km_ref[rows, sl] = (kn[:, sl] + kpe).astype(BF16)
        vm_ref[rows, :] = _dot(ckv, wuv_ref[...]).astype(BF16)

        pq = _dot(h, win_ref[:, 512:1024])
        for j in range(4):
            sl = slice(j * LANES, (j + 1) * LANES)
            qs_ref[rows, sl] = (_rope(pq[:, sl], sc, sa, sb, 96, 32) * swa_scale).astype(BF16)
        pkv = _dot(h, win_ref[:, 1024:1280])
        ks_ref[rows, :] = _rope(pkv[:, 0:128], sc, sa, sb, 96, 32).astype(BF16)
        vs_ref[rows, :] = pkv[:, 128:256].astype(BF16)


def _pre_even(rows, x, mod, g, win, qg, wuq, kvg, wuk, wuv, tab):
    n = rows.n
    widths = (1024, 1024, 512, 512, 128, 128)
    return pl.pallas_call(
        _pre_even_kernel,
        grid=rows.grid,
        in_specs=[
            rows.row_spec(D_MODEL), rows.mod_spec(), _const_spec((1, D_MODEL)),
            _const_spec(win.shape), _const_spec(qg.shape), _const_spec(wuq.shape),
            _const_spec(kvg.shape), _const_spec(wuk.shape), _const_spec(wuv.shape),
            rows.table_spec(768),
        ],
        out_specs=[rows.row_spec(w) for w in widths],
        out_shape=[jax.ShapeDtypeStruct((n, w), BF16) for w in widths],
        compiler_params=_params(),
        name="pre_even",
    )(x, mod, g, win, qg, wuq, kvg, wuk, wuv, tab)


def _pre_odd_kernel(x_ref, mod_ref, g_ref, win_ref, qg_ref, kg_ref, tab_ref,
                    qn_ref, kn_ref, vn_ref, qq_ref, kq_ref, vq_ref):
    na_scale = LOG2E * NA_DIM ** -0.5
    gqa_scale = LOG2E * GQA_DIM ** -0.5
    for rows in _row_halves(x_ref.shape[0]):
        h = _modulated(x_ref, mod_ref, g_ref, rows, 0, 1).astype(BF16)
        gc, gs = tab_ref[rows, 0:128], tab_ref[rows, 128:256]

        def normed(blk, gain, gc=gc, gs=gs):
            y = _rms(blk, gain)
            return y * gc + pltpu.roll(y, 64, axis=1) * gs

        qn_ref[rows, :] = (_dot(h, win_ref[:, 0:512]) * na_scale).astype(BF16)
        kn_ref[rows, :] = _dot(h, win_ref[:, 512:1024]).astype(BF16)
        vn_ref[rows, :] = _dot(h, win_ref[:, 1024:1536]).astype(BF16)
        pq = _dot(h, win_ref[:, 1536:2048])
        for j in range(GQA_HEADS):
            sl = slice(j * LANES, (j + 1) * LANES)
            qq_ref[rows, sl] = (normed(pq[:, sl], qg_ref[...]) * gqa_scale).astype(BF16)
        pkv = _dot(h, win_ref[:, 2048:2560])
        for j in range(GQA_KV_HEADS):
            sl = slice(j * LANES, (j + 1) * LANES)
            kq_ref[rows, sl] = normed(pkv[:, sl], kg_ref[...]).astype(BF16)
        vq_ref[rows, :] = pkv[:, 256:512].astype(BF16)


def _pre_odd(rows, x, mod, g, win, qg, kg, tab):
    n = rows.n
    widths = (512, 512, 512, 512, 256, 256)
    return pl.pallas_call(
        _pre_odd_kernel,
        grid=rows.grid,
        in_specs=[
            rows.row_spec(D_MODEL), rows.mod_spec(), _const_spec((1, D_MODEL)),
            _const_spec(win.shape), _const_spec(qg.shape), _const_spec(kg.shape),
            rows.table_spec(256),
        ],
        out_specs=[rows.row_spec(w) for w in widths],
        out_shape=[jax.ShapeDtypeStruct((n, w), BF16) for w in widths],
        compiler_params=_params(),
        name="pre_odd",
    )(x, mod, g, win, qg, kg, tab)


def _attend(q, chunks, sink=None):
    return _attend_many([(q, chunks, sink)])[0]


def _attend_many(streams):
    state = []
    for q, _, sink in streams:
        if sink is None:
            state.append((None, None))
        else:
            rows = q.shape[0]
            state.append((sink, jnp.concatenate([jnp.zeros((rows, LANES), F32), jnp.ones((rows, LANES), F32)],
                                                axis=1)))
    for c in range(max(len(chunks) for _, chunks, _ in streams)):
        for idx, (q, chunks, _) in enumerate(streams):
            if c >= len(chunks):
                continue
            k, v, bias = chunks[c]
            m, acc = state[idx]
            s = _dot_nt(q, k)
            if bias is not None:
                s = s + bias
            cm = jnp.max(s, axis=-1, keepdims=True)
            m_new = cm if m is None else jnp.maximum(m, cm)
            p = jnp.exp2(s - m_new).astype(BF16)
            vx = jnp.concatenate([v, jnp.ones(v.shape, BF16)], axis=1)
            pv = _dot(p, vx)
            acc = pv if acc is None else jnp.exp2(m - m_new) * acc + pv
            state[idx] = (m_new, acc)
    return [acc[:, :LANES] / acc[:, LANES:] for _, acc in state]


def _chunks(k_ref, v_ref, lanes, size):
    n = k_ref.shape[0]
    assert n % size == 0
    return [(k_ref[c * size:(c + 1) * size, lanes], v_ref[c * size:(c + 1) * size, :], None)
            for c in range(n // size)]


def _mla_kernel(has_lat, q_ref, kc_ref, vc_ref, *rest):
    if has_lat:
        kl_ref, vl_ref, o_ref = rest
    else:
        (o_ref,) = rest
    streams = []
    n_pairs = o_ref.shape[1] // LANES
    for e in range(2 * n_pairs):
        sl = slice(e * LANES, (e + 1) * LANES)
        vl_lanes = slice((e // 2) * LANES, (e // 2 + 1) * LANES)
        chunks = [(kc_ref[:, sl], vc_ref[:, vl_lanes], None)]
        if has_lat:
            chunks += [(k, v[:, vl_lanes], b) for k, v, b in _chunks(kl_ref, vl_ref, sl, KEY_CHUNK)]
        streams.append((q_ref[:, sl], chunks, None))
    outs = _attend_many(streams)
    lo = _lane_lo(outs[0].shape)
    for j in range(n_pairs):
        o_ref[:, j * LANES:(j + 1) * LANES] = jnp.where(lo, outs[2 * j], outs[2 * j + 1]).astype(BF16)


def _mla_attn(q, kc, vc, kl, vl, batch, q_seq, ctx_len, lat_seq, tq):
    has_lat = kl is not None
    nq = q_seq // tq
    pairs = 1 if has_lat else MLA_HEADS // 2
    in_specs = [
        pl.BlockSpec((tq, 256 * pairs), lambda b, j, t: (b * nq + t, j)),
        pl.BlockSpec((ctx_len, 256 * pairs), lambda b, j, t: (b, j)),
        pl.BlockSpec((ctx_len, 128 * pairs), lambda b, j, t: (b, j)),
    ]
    args = [q, kc, vc]
    if has_lat:
        in_specs += [
            pl.BlockSpec((lat_seq, 256), lambda b, j, t: (b, j)),
            pl.BlockSpec((lat_seq, 128), lambda b, j, t: (b, j)),
        ]
        args += [kl, vl]
    return pl.pallas_call(
        functools.partial(_mla_kernel, has_lat),
        grid=(batch, MLA_HEADS // 2 // pairs, nq),
        in_specs=in_specs,
        out_specs=pl.BlockSpec((tq, 128 * pairs), lambda b, j, t: (b * nq + t, j)),
        out_shape=jax.ShapeDtypeStruct((batch * q_seq, MLA_HEADS * MLA_V), BF16),
        compiler_params=_params(),
        name="mla_attn_lat" if has_lat else "mla_attn_ctx",
    )(*args)


def _gqa_kernel(q_ref, kc_ref, vc_ref, kl_ref, vl_ref, o_ref):
    tq = q_ref.shape[0]
    per_group = GQA_HEADS // GQA_KV_HEADS
    q = jnp.concatenate([q_ref[:, e * LANES:(e + 1) * LANES] for e in range(per_group)], axis=0)
    every = slice(None)
    chunks = _chunks(kc_ref, vc_ref, every, kc_ref.shape[0]) + _chunks(kl_ref, vl_ref, every, KEY_CHUNK)
    o = _attend(q, chunks).astype(BF16)
    for e in range(per_group):
        o_ref[:, e * LANES:(e + 1) * LANES] = o[e * tq:(e + 1) * tq]


def _gqa_attn(q, kc, vc, kl, vl, batch, seq, ctx_len, tq):
    nq = seq // tq
    return pl.pallas_call(
        _gqa_kernel,
        grid=(batch, GQA_KV_HEADS, nq),
        in_specs=[
            pl.BlockSpec((tq, 256), lambda b, g, t: (b * nq + t, g)),
            pl.BlockSpec((ctx_len, 128), lambda b, g, t: (b, g)),
            pl.BlockSpec((ctx_len, 128), lambda b, g, t: (b, g)),
            pl.BlockSpec((seq, 128), lambda b, g, t: (b, g)),
            pl.BlockSpec((seq, 128), lambda b, g, t: (b, g)),
        ],
        out_specs=pl.BlockSpec((tq, 256), lambda b, g, t: (b * nq + t, g)),
        out_shape=jax.ShapeDtypeStruct((batch * seq, GQA_HEADS * GQA_DIM), BF16),
        compiler_params=_params(),
        name="gqa_attn",
    )(q, kc, vc, kl, vl)


SWA_QB = 256
SWA_GROUP = SWA_HEADS // SWA_KV_HEADS


def _swa_group_lhs(q_ref, r0, rows, g):
    lo = _lane_lo((rows, LANES))
    keep = lo if g == 0 else jnp.logical_not(lo)
    blocks = [jnp.where(keep, q_ref[r0:r0 + rows, j * LANES:(j + 1) * LANES], jnp.zeros((), BF16))
              for j in range(SWA_GROUP)]
    return jnp.concatenate(blocks, axis=0)


def _swa_sink_col(sink_ref, g, rows):
    blk = lax.broadcasted_iota(jnp.int32, (SWA_GROUP * rows, 1), 0) // rows
    col = jnp.full((SWA_GROUP * rows, 1), sink_ref[g * SWA_GROUP], F32)
    for j in range(1, SWA_GROUP):
        col = jnp.where(blk == j, sink_ref[g * SWA_GROUP + j], col)
    return col * LOG2E


def _swa_store(o_ref, r0, rows, o_groups):
    lo = _lane_lo((rows, LANES))
    for j in range(SWA_GROUP):
        a = o_groups[0][j * rows:(j + 1) * rows]
        b = o_groups[1][j * rows:(j + 1) * rows]
        o_ref[r0:r0 + rows, j * LANES:(j + 1) * LANES] = jnp.where(lo, a, b).astype(BF16)


SWA_BAND = SWA_QB + 2 * SWA_WINDOW


def _swa_mask_table():
    ii = np.arange(SWA_GROUP * SWA_QB)[:, None] % SWA_QB
    jj = np.arange(SWA_BAND)[None, :]
    in_window = np.abs(jj - SWA_WINDOW - ii) <= SWA_WINDOW
    first = in_window & (jj >= SWA_WINDOW)
    last = in_window & (jj < SWA_WINDOW + SWA_QB)
    return jnp.asarray(np.where(np.stack([first, in_window, last]), 0.0, NEG_INF), F32)


def _swa_lat_kernel(tq, sink_ref, mask_ref, q_ref, kp_ref, kt_ref, kn_ref, vp_ref, vt_ref, vn_ref,
                    kc_ref, vc_ref, o_ref):
    t = pl.program_id(1)
    kband = jnp.concatenate([kp_ref[...], kt_ref[...], kn_ref[...]], axis=0)
    vband = jnp.concatenate([vp_ref[...], vt_ref[...], vn_ref[...]], axis=0)
    band = SWA_BAND
    n_blocks = tq // SWA_QB
    for i in range(n_blocks):
        variant = 1
        if i == 0:
            variant = jnp.where(t == 0, 0, variant)
        if i == n_blocks - 1:
            variant = jnp.where(t == pl.num_programs(1) - 1, 2, variant)
        bias = mask_ref[variant]
        kb = kband[i * SWA_QB:i * SWA_QB + band]
        vb = vband[i * SWA_QB:i * SWA_QB + band]
        o_groups = []
        for g in range(SWA_KV_HEADS):
            lhs = _swa_group_lhs(q_ref, i * SWA_QB, SWA_QB, g)
            o_groups.append(_attend(lhs, [(kc_ref[...], vc_ref[...], None), (kb, vb, bias)],
                                    _swa_sink_col(sink_ref, g, SWA_QB)))
        _swa_store(o_ref, i * SWA_QB, SWA_QB, o_groups)


def _swa_ctx_kernel(ctx_len, sink_ref, q_ref, kc_ref, vc_ref, o_ref):
    o_groups = []
    for g in range(SWA_KV_HEADS):
        lhs = _swa_group_lhs(q_ref, 0, ctx_len, g)
        o_groups.append(_attend(lhs, [(kc_ref[...], vc_ref[...], None)], _swa_sink_col(sink_ref, g, ctx_len)))
    _swa_store(o_ref, 0, ctx_len, o_groups)


def _swa_attn_lat(sink, q, k, v, kc, vc, batch, seq, ctx_len, tq):
    nq = seq // tq
    nb = seq // SWA_WINDOW
    per = tq // SWA_WINDOW
    prev_spec = pl.BlockSpec((SWA_WINDOW, 128), lambda b, t: (b * nb + jnp.maximum(t * per - 1, 0), 0))
    next_spec = pl.BlockSpec((SWA_WINDOW, 128), lambda b, t: (b * nb + jnp.minimum(t * per + per, nb - 1), 0))
    tile_spec = pl.BlockSpec((tq, 128), lambda b, t: (b * nq + t, 0))
    ctx_spec = pl.BlockSpec((ctx_len, 128), lambda b, t: (b, 0))
    mask = _swa_mask_table()
    return pl.pallas_call(
        functools.partial(_swa_lat_kernel, tq),
        grid=(batch, nq),
        in_specs=[
            pl.BlockSpec(memory_space=pltpu.SMEM),
            pl.BlockSpec(mask.shape, lambda b, t: (0, 0, 0)),
            pl.BlockSpec((tq, 512), lambda b, t: (b * nq + t, 0)),
            prev_spec, tile_spec, next_spec, prev_spec, tile_spec, next_spec,
            ctx_spec, ctx_spec,
        ],
        out_specs=pl.BlockSpec((tq, 512), lambda b, t: (b * nq + t, 0)),
        out_shape=jax.ShapeDtypeStruct((batch * seq, SWA_HEADS * SWA_DIM), BF16),
        compiler_params=_params(),
        name="swa_attn_lat",
    )(sink, mask, q, k, k, k, v, v, v, kc, vc)


def _swa_attn_ctx(sink, q, kc, vc, batch, ctx_len):
    return pl.pallas_call(
        functools.partial(_swa_ctx_kernel, ctx_len),
        grid=(batch,),
        in_specs=[
            pl.BlockSpec(memory_space=pltpu.SMEM),
            pl.BlockSpec((ctx_len, 512), lambda b: (b, 0)),
            pl.BlockSpec((ctx_len, 128), lambda b: (b, 0)),
            pl.BlockSpec((ctx_len, 128), lambda b: (b, 0)),
        ],
        out_specs=pl.BlockSpec((ctx_len, 512), lambda b: (b, 0)),
        out_shape=jax.ShapeDtypeStruct((batch * ctx_len, SWA_HEADS * SWA_DIM), BF16),
        compiler_params=_params(),
        name="swa_attn_ctx",
    )(sink, q, kc, vc)


NA_TILES_PER_STEP = 8


def _na_kernel(n_tiles, tab_ref, q_ref, k_ref, v_ref, kc_ref, vc_ref, o_ref):
    grid_rows = n_tiles * NA_TILE_ROWS
    tq = NA_TILE_ROWS * GRID_W
    lo = _lane_lo((tq, LANES))
    streams = []
    for u in range(NA_TILES_PER_STEP):
        t = pl.program_id(2) * NA_TILES_PER_STEP + u
        ws = jnp.clip(t * NA_TILE_ROWS - NA_KH // 2, 0, grid_rows - NA_WIN_ROWS)
        variant = jnp.where(t == 0, 0, jnp.where(t == n_tiles - 1, 2, 1))
        start = pl.multiple_of(ws * GRID_W, GRID_W)
        kwin = k_ref[pl.ds(start, NA_WIN_ROWS * GRID_W), :]
        vwin = v_ref[pl.ds(start, NA_WIN_ROWS * GRID_W), :]
        q = q_ref[u * tq:(u + 1) * tq, :]
        zero = jnp.zeros((), BF16)
        lhs = jnp.concatenate([jnp.where(lo, q, zero), jnp.where(lo, zero, q)], axis=0)
        bias = tab_ref[variant].reshape(2 * tq, NA_WIN_ROWS * GRID_W)
        streams.append((lhs, [(kc_ref[...], vc_ref[...], None), (kwin, vwin, bias)], None))
    for u, o in enumerate(_attend_many(streams)):
        o_ref[u * tq:(u + 1) * tq, :] = jnp.where(lo, o[:tq], o[tq:]).astype(BF16)


def _na_attn(tab, q, k, v, kc, vc, batch, seq, ctx_len):
    tq = NA_TILE_ROWS * GRID_W
    n_tiles = seq // tq
    nk = NA_WIN_ROWS * GRID_W
    assert n_tiles % NA_TILES_PER_STEP == 0
    n_steps = n_tiles // NA_TILES_PER_STEP
    tq_step = tq * NA_TILES_PER_STEP
    return pl.pallas_call(
        functools.partial(_na_kernel, n_tiles),
        grid=(NA_HEADS // 2, batch, n_steps),
        in_specs=[
            pl.BlockSpec((3, 2, tq, nk), lambda j, b, t: (0, j, 0, 0)),
            pl.BlockSpec((tq_step, 128), lambda j, b, t: (b * n_steps + t, j)),
            pl.BlockSpec((seq, 128), lambda j, b, t: (b, j)),
            pl.BlockSpec((seq, 128), lambda j, b, t: (b, j)),
            pl.BlockSpec((ctx_len, 128), lambda j, b, t: (b, j)),
            pl.BlockSpec((ctx_len, 128), lambda j, b, t: (b, j)),
        ],
        out_specs=pl.BlockSpec((tq_step, 128), lambda j, b, t: (b * n_steps + t, j)),
        out_shape=jax.ShapeDtypeStruct((batch * seq, NA_HEADS * NA_DIM), BF16),
        compiler_params=_params(),
        name="na_attn",
    )(tab, q, k, v, kc, vc)


def _na_bias_table(rpb, grid_rows):
    cols = np.arange(GRID_W)
    col_start = np.clip(cols - NA_KW // 2, 0, GRID_W - NA_KW)
    col_ok = (cols[None, :] >= col_start[:, None]) & (cols[None, :] < col_start[:, None] + NA_KW)
    dc = np.clip(cols[None, :] - cols[:, None], -(NA_KW - 1), NA_KW - 1) + NA_KW - 1
    n_heads, n_dr = rpb.shape[0], rpb.shape[1]
    by_col = jnp.take(rpb.astype(F32) * LOG2E, jnp.asarray(dc.reshape(-1)), axis=2).reshape(n_heads, n_dr, GRID_W, GRID_W)
    by_col = jnp.where(jnp.asarray(col_ok)[None, None], by_col, NEG_INF)
    by_col = jnp.concatenate([by_col, jnp.full((n_heads, 1, GRID_W, GRID_W), NEG_INF, F32)], axis=1)
    n_tiles = grid_rows // NA_TILE_ROWS
    variants = []
    for t in (0, min(2, n_tiles - 1), n_tiles - 1):
        r0 = t * NA_TILE_ROWS
        ws = int(np.clip(r0 - NA_KH // 2, 0, grid_rows - NA_WIN_ROWS))
        r = r0 + np.arange(NA_TILE_ROWS)
        rs = np.clip(r - NA_KH // 2, 0, grid_rows - NA_KH)
        key_row = ws + np.arange(NA_WIN_ROWS)
        row_ok = (key_row[None, :] >= rs[:, None]) & (key_row[None, :] < rs[:, None] + NA_KH)
        dr = np.where(row_ok, key_row[None, :] - r[:, None] + NA_KH - 1, n_dr)
        rows = []
        for i in range(NA_TILE_ROWS):
            rows.append(jnp.concatenate([by_col[:, int(d)] for d in dr[i]], axis=-1))
        variants.append(jnp.concatenate(rows, axis=1))
    return jnp.stack(variants, axis=0)


def _gelu_tanh(x):
    return x * (0.5 * (1.0 + jnp.tanh(math.sqrt(2.0 / math.pi) * (x + 0.044715 * (x * x * x)))))


def _mix_ffn_kernel(tm, tps, a_ref, b_ref, x_ref, modp_ref, modf_ref, wa_ref, wb_ref, g1_ref, g2_ref, g3_ref,
                    wg_ref, wv_ref, cw_ref, wd_ref, o_ref, hh_ref, x1_ref, ug_ref, uv_ref, act_ref, acc_ref):
    i = pl.program_id(0)
    zero = jnp.zeros((), BF16)

    @pl.when(i == 0)
    def _():
        hh_ref[...] = jnp.zeros_like(hh_ref)
        x1_ref[...] = jnp.zeros_like(x1_ref)

    def step(s):
        o = 1 - s
        y = _dot(a_ref[...], wa_ref[...]) + _dot(b_ref[...], wb_ref[...])
        x1 = x_ref[...] + modp_ref[0, 2:3, :] * _rms(y, g1_ref[...])
        h = (_rms(x1, g2_ref[...]) * (1.0 + modp_ref[0, 4:5, :]) + modp_ref[0, 3:4, :]).astype(BF16)
        x1_ref[s] = x1
        hh_ref[s, HALO:HALO + tm] = h
        first = i % tps == 0
        hh_ref[s, 0:HALO] = jnp.where(first, zero, hh_ref[o, tm:tm + HALO])
        hh_ref[o, HALO + tm:] = jnp.where(first, zero, h[0:HALO])

        def up(c, slot):
            uv_ref[slot] = _dot(hh_ref[o, HALO:HALO + tm], wv_ref[c])
            ug_ref[slot] = _dot(hh_ref[o], wg_ref[c])

        def activate(c, slot):
            cw = cw_ref[c]
            gate = cw[3:4]
            for tap in range(3):
                gate = gate + ug_ref[slot, HALO - 1 + tap:HALO - 1 + tap + tm] * cw[tap:tap + 1]
            act_ref[c] = (_gelu_tanh(gate) * uv_ref[slot]).astype(BF16)

        up(0, 0)
        for c in range(N_FF_CHUNKS):
            if c + 1 < N_FF_CHUNKS:
                up(c + 1, (c + 1) % 2)
            activate(c, c % 2)

        for nb in range(D_MODEL // FF_CHUNK):
            cols = slice(nb * FF_CHUNK, (nb + 1) * FF_CHUNK)
            f = _dot(act_ref[0], wd_ref[0, :, cols])
            for c in range(1, N_FF_CHUNKS):
                f = f + _dot(act_ref[c], wd_ref[c, :, cols])
            acc_ref[:, cols] = f
        o_ref[...] = x1_ref[o] + modf_ref[0, 5:6, :] * _rms(acc_ref[...], g3_ref[...])

    pl.when(i % 2 == 0)(lambda: step(0))
    pl.when(i % 2 == 1)(lambda: step(1))


def _mix_ffn(rows, a, b, x, mod, wa, wb, g1, g2, g3, wg, wv, cw, wd):
    n, tm, tps = rows.n, rows.tm, rows.tps
    n_tiles = n // tm
    assert n_tiles % tps == 0

    def tile_in(width):
        return pl.BlockSpec((tm, width), lambda i: (jnp.minimum(i, n_tiles - 1), 0))

    def mod_spec(tile_of_step):
        if rows.per_batch_mod:
            return pl.BlockSpec((1, 8, D_MODEL), lambda i: (tile_of_step(i) // tps, 0, 0))
        return pl.BlockSpec((1, 8, D_MODEL), lambda i: (0, 0, 0))

    row = _const_spec((1, D_MODEL))
    return pl.pallas_call(
        functools.partial(_mix_ffn_kernel, tm, tps),
        grid=(n_tiles + 1,),
        in_specs=[
            tile_in(a.shape[1]), tile_in(b.shape[1]), tile_in(D_MODEL),
            mod_spec(lambda i: jnp.minimum(i, n_tiles - 1)), mod_spec(lambda i: jnp.maximum(i - 1, 0)),
            _const_spec(wa.shape), _const_spec(wb.shape), row, row, row,
            _const_spec(wg.shape), _const_spec(wv.shape), _const_spec(cw.shape), _const_spec(wd.shape),
        ],
        out_specs=pl.BlockSpec((tm, D_MODEL), lambda i: (jnp.maximum(i - 1, 0), 0)),
        out_shape=jax.ShapeDtypeStruct((n, D_MODEL), F32),
        scratch_shapes=[
            pltpu.VMEM((2, tm + 2 * HALO, D_MODEL), BF16),
            pltpu.VMEM((2, tm, D_MODEL), F32),
            pltpu.VMEM((2, tm + 2 * HALO, FF_CHUNK), F32),
            pltpu.VMEM((2, tm, FF_CHUNK), F32),
            pltpu.VMEM((N_FF_CHUNKS, tm, FF_CHUNK), BF16),
            pltpu.VMEM((tm, D_MODEL), F32),
        ],
        compiler_params=_params(),
        name="mix_ffn",
    )(a, b, x, mod, mod, wa, wb, g1, g2, g3, wg, wv, cw, wd)


def _rope_angles(n, d):
    t = jnp.arange(n)
    rows = (t // GRID_W).astype(F32)
    cols = (t % GRID_W).astype(F32)
    freqs = ROPE_BASE ** (-jnp.arange(d // 4, dtype=F32) * 4.0 / d)
    ang = jnp.concatenate([rows[:, None] * freqs, cols[:, None] * freqs], axis=-1)
    return jnp.cos(ang), jnp.sin(ang)


def _rope_lane_tables(cos, sin, offset, group):
    n, half = cos.shape
    c = jnp.ones((n, group), F32)
    sa = jnp.zeros((n, group), F32)
    sb = jnp.zeros((n, group), F32)
    c = c.at[:, offset:offset + half].set(cos).at[:, offset + half:offset + 2 * half].set(cos)
    sa = sa.at[:, offset:offset + half].set(-sin)
    sb = sb.at[:, offset + half:offset + 2 * half].set(sin)
    reps = LANES // group
    return tuple(jnp.tile(a, (1, reps)) for a in (c, sa, sb))


def _identity_tables(n, width_blocks):
    ones = jnp.ones((n, LANES), F32)
    zeros = jnp.zeros((n, LANES), F32)
    return [ones, zeros, zeros][:width_blocks]


def _even_weights(w_in, w_out, w_uq, w_ukv):
    z = lambda r, c: jnp.zeros((r, c), F32)
    mla_in = MLA_Q_RANK + MLA_KV_RANK + MLA_ROPE
    kpe = w_in[:, MLA_Q_RANK + MLA_KV_RANK:mla_in]
    swa = w_in[:, mla_in:]
    swa_q = swa[:, :SWA_HEADS * SWA_DIM].reshape(D_MODEL, SWA_HEADS, SWA_DIM)
    pair_order = np.array([h for j in range(SWA_GROUP) for h in (j, SWA_GROUP + j)])
    swa_q = swa_q[:, pair_order, :].reshape(D_MODEL, SWA_HEADS * SWA_DIM)
    win = jnp.concatenate([
        w_in[:, :MLA_Q_RANK + MLA_KV_RANK], z(D_MODEL, 64), kpe, z(D_MODEL, 32), swa_q,
        swa[:, SWA_HEADS * SWA_DIM:]], axis=1)
    assert win.shape[1] == EVEN_COLS
    wuq = w_uq.reshape(MLA_Q_RANK, MLA_HEADS, MLA_NOPE + MLA_ROPE)
    wuq = jnp.pad(wuq, ((0, 0), (0, 0), (0, LANES - MLA_NOPE - MLA_ROPE))).reshape(MLA_Q_RANK, MLA_HEADS * LANES)
    wukv = w_ukv.reshape(MLA_KV_RANK, MLA_HEADS, MLA_NOPE + MLA_V)
    wuk = jnp.pad(wukv[:, :, :MLA_NOPE], ((0, 0), (0, 0), (0, LANES - MLA_NOPE))).reshape(MLA_KV_RANK, MLA_HEADS * LANES)
    wuv = wukv[:, :, MLA_NOPE:].reshape(MLA_KV_RANK, MLA_HEADS * MLA_V)
    n_a = MLA_HEADS * MLA_V
    wo_a = w_out[:n_a]
    wo_b = w_out[n_a:].reshape(SWA_HEADS, SWA_DIM, D_MODEL)[pair_order].reshape(SWA_HEADS * SWA_DIM, D_MODEL)
    return tuple(a.astype(BF16) for a in (win, wuq, wuk, wuv, wo_a, wo_b))


def _ffn_weights(w_up, conv_w, conv_b, w_down):
    def chunks(w):
        return w.reshape(D_MODEL, N_FF_CHUNKS, FF_CHUNK).transpose(1, 0, 2).astype(BF16)

    cw = jnp.concatenate([conv_w, conv_b[None], jnp.zeros((4, D_FF), F32)], axis=0)
    cw = cw.reshape(8, N_FF_CHUNKS, FF_CHUNK).transpose(1, 0, 2)
    wd = w_down.reshape(N_FF_CHUNKS, FF_CHUNK, D_MODEL).astype(BF16)
    return chunks(w_up[:, :D_FF]), chunks(w_up[:, D_FF:]), cw, wd


def kernel(x, c, ctx, c_ctx, ada_w, ada_b, pre_mix_g, post_mix_g, pre_ffn_g, post_ffn_g, ffn_up, ffn_conv_w,
           ffn_conv_b, ffn_down, even_w_in, even_w_out, mla_q_norm_g, mla_w_uq, mla_kv_norm_g, mla_w_ukv,
           swa_sink, odd_w_in, odd_w_out, na_rpb, gqa_q_g, gqa_k_g):
    batch, seq, _ = x.shape
    ctx_len = ctx.shape[1]
    depth = ada_w.shape[0]
    n_lat, n_ctx = batch * seq, batch * ctx_len

    pad_rows = -(batch + 1) % 8
    cc = jnp.concatenate([c, c_ctx[None], jnp.zeros((pad_rows, D_MODEL), F32)], axis=0)
    mods = _ada(cc, ada_w, ada_b).reshape(depth, cc.shape[0], 6, D_MODEL)
    mods = jnp.pad(mods, ((0, 0), (0, 0), (0, 2), (0, 0)))

    lat_rows = _Rows(n_lat, seq, TILE_MIX_FFN, True)
    pre_rows = _Rows(n_lat, seq, 2 * PRE_HALF, True)
    ctx_rows = _Rows(n_ctx, ctx_len, ctx_len, False)
    pre_ctx_tile = 2 * PRE_HALF if n_ctx % (2 * PRE_HALF) == 0 else ctx_len
    pre_ctx_rows = _Rows(n_ctx, pre_ctx_tile, pre_ctx_tile, False)

    cos_m, sin_m = _rope_angles(seq, MLA_ROPE)
    cos_s, sin_s = _rope_angles(seq, SWA_DIM)
    cos_g, sin_g = _rope_angles(seq, GQA_DIM)
    gc, gsa, gsb = _rope_lane_tables(cos_g, sin_g, 0, LANES)
    tab_even = jnp.concatenate(_rope_lane_tables(cos_m, sin_m, MLA_NOPE, LANES)
                               + _rope_lane_tables(cos_s, sin_s, 0, SWA_DIM), axis=1)
    tab_odd = jnp.concatenate([gc, gsa + gsb], axis=1)
    tab_even_ctx = jnp.concatenate(_identity_tables(pre_ctx_tile, 3) * 2, axis=1)
    tab_odd_ctx = jnp.concatenate(_identity_tables(pre_ctx_tile, 2), axis=1)

    x = x.reshape(n_lat, D_MODEL)
    ctx = ctx.reshape(n_ctx, D_MODEL)
    row = lambda v: v.reshape(1, -1)

    for layer in range(depth):
        need_ctx = layer < depth - 1
        i = layer // 2
        mod_lat = mods[layer, :batch]
        mod_ctx = mods[layer, batch:batch + 1]
        if layer % 2 == 0:
            win, wuq, wuk, wuv, wo_a, wo_b = _even_weights(even_w_in[i], even_w_out[i], mla_w_uq[i], mla_w_ukv[i])
            shared = (row(pre_mix_g[layer]), win, row(mla_q_norm_g[i]), wuq, row(mla_kv_norm_g[i]), wuk, wuv)
            qm, km, vm, qs, ks, vs = _pre_even(pre_rows, x, mod_lat, *shared, tab_even)
            qmc, kmc, vmc, qsc, ksc, vsc = _pre_even(pre_ctx_rows, ctx, mod_ctx, *shared, tab_even_ctx)
            a_lat = _mla_attn(qm, kmc, vmc, km, vm, batch, seq, ctx_len, seq, TILE_MLA_Q)
            b_lat = _swa_attn_lat(swa_sink[i], qs, ks, vs, ksc, vsc, batch, seq, ctx_len, TILE_SWA_Q)
            if need_ctx:
                a_ctx = _mla_attn(qmc, kmc, vmc, None, None, batch, ctx_len, ctx_len, seq, ctx_len)
                b_ctx = _swa_attn_ctx(swa_sink[i], qsc, ksc, vsc, batch, ctx_len)
        else:
            win = odd_w_in[i].astype(BF16)
            n_a = NA_HEADS * NA_DIM
            wo_a, wo_b = odd_w_out[i][:n_a].astype(BF16), odd_w_out[i][n_a:].astype(BF16)
            shared = (row(pre_mix_g[layer]), win, row(gqa_q_g[i]), row(gqa_k_g[i]))
            qn, kn, vn, qq, kq, vq = _pre_odd(pre_rows, x, mod_lat, *shared, tab_odd)
            qnc, knc, vnc, qqc, kqc, vqc = _pre_odd(pre_ctx_rows, ctx, mod_ctx, *shared, tab_odd_ctx)
            na_tab = _na_bias_table(na_rpb[i], seq // GRID_W)
            a_lat = _na_attn(na_tab, qn, kn, vn, knc, vnc, batch, seq, ctx_len)
            b_lat = _gqa_attn(qq, kqc, vqc, kq, vq, batch, seq, ctx_len, TILE_GQA_Q)
            assert not need_ctx, "context-query attention for odd layers is only needed when depth > 2"
        wg, wv, cw, wd = _ffn_weights(ffn_up[layer], ffn_conv_w[layer], ffn_conv_b[layer], ffn_down[layer])
        g1, g2, g3 = row(post_mix_g[layer]), row(pre_ffn_g[layer]), row(post_ffn_g[layer])
        x = _mix_ffn(lat_rows, a_lat, b_lat, x, mod_lat, wo_a, wo_b, g1, g2, g3, wg, wv, cw, wd)
        if need_ctx:
            ctx = _mix_ffn(ctx_rows, a_ctx, b_ctx, ctx, mod_ctx, wo_a, wo_b, g1, g2, g3, wg, wv, cw, wd)
    return x.reshape(batch, seq, D_MODEL)
```
